```python
import math
import jax, jax.numpy as jnp
from jax import lax
import numpy as np

D_MODEL = 1024
BATCH = 8
SEQ = 2048
DEPTH = 4
DEC_BATCH = 128
DEC_SEQ = 8
PAST_LEN = 8192
PAGE_SIZE = 128

N_EVEN = (DEPTH + 1) // 2
N_ODD = DEPTH // 2
CONV_A_DIM = D_MODEL // 2
CONV_A_WIDTH = 31
HEAD_DIM = 64
N_Q_HEADS = (D_MODEL // 2) // HEAD_DIM
N_KV_HEADS = N_Q_HEADS // 4
WINDOW = 128
ROPE_DIM = HEAD_DIM // 4
ROPE_THETA = 500000.0
IN_EVEN = 2 * CONV_A_DIM + (N_Q_HEADS + 2 * N_KV_HEADS) * HEAD_DIM
OUT_EVEN_IN = CONV_A_DIM + N_Q_HEADS * HEAD_DIM
D_INNER = 2 * D_MODEL
SSM_HEAD_DIM = 64
SSM_HEADS = D_INNER // SSM_HEAD_DIM
SSM_STATE = 128
SSM_GROUPS = 4
SSM_CONV_WIDTH = 4
SSM_CONV_DIM = D_INNER + 2 * SSM_GROUPS * SSM_STATE
IN_ODD = D_INNER + SSM_CONV_DIM + SSM_HEADS
SSD_CHUNK = 128
D_FF = ((8 * D_MODEL // 3 + 255) // 256) * 256
FFN_CONV_WIDTH = 3
EPS = 1e-6

kernel_name = "hybrid_conformer_swa_mamba2_convffn_step"


def rmsnorm(x, g):
    xf = x.astype(jnp.float32)
    xf = xf * lax.rsqrt(jnp.mean(xf * xf, axis=-1, keepdims=True) + EPS)
    return xf.astype(x.dtype) * g


def layernorm(x, g, b):
    xf = x.astype(jnp.float32)
    mu = jnp.mean(xf, axis=-1, keepdims=True)
    var = jnp.mean(jnp.square(xf - mu), axis=-1, keepdims=True)
    return ((xf - mu) * lax.rsqrt(var + EPS)).astype(x.dtype) * g + b


def causal_dwconv(padded, w, b):
    c = padded.shape[-1]
    y = lax.conv_general_dilated(padded, w[:, None, :], (1,), 'VALID',
                                 dimension_numbers=('NWC', 'WIO', 'NWC'), feature_group_count=c)
    return y + b


def partial_rope(x, pos):
    half = ROPE_DIM // 2
    inv_freq = ROPE_THETA ** (-jnp.arange(0, ROPE_DIM, 2, dtype=jnp.float32) / ROPE_DIM)
    ang = pos.astype(jnp.float32)[:, None] * inv_freq[None, :]
    cos = jnp.cos(ang)[:, None, :]
    sin = jnp.sin(ang)[:, None, :]
    xr = x[..., :ROPE_DIM].astype(jnp.float32)
    x1, x2 = xr[..., :half], xr[..., half:]
    rot = jnp.concatenate([x1 * cos - x2 * sin, x2 * cos + x1 * sin], axis=-1).astype(x.dtype)
    return jnp.concatenate([rot, x[..., ROPE_DIM:]], axis=-1)


def sink_attention(q, k, v, q_pos, k_pos, sinks):
    b, n, tq, h, d = q.shape
    kv = k.shape[3]
    g = h // kv
    qg = q.reshape(b, n, tq, kv, g, d)
    s = jnp.einsum('bnqkgd,bnskd->bnkgqs', qg, k).astype(jnp.float32) * (d ** -0.5)
    rel = q_pos[:, :, None] - k_pos[:, None, :]
    valid = (rel >= 0) & (rel < WINDOW) & (k_pos[:, None, :] >= 0)
    s = jnp.where(valid[None, :, None, None], s, -jnp.inf)
    sink = jnp.broadcast_to(sinks.astype(jnp.float32).reshape(kv, g)[None, None, :, :, None, None],
                            s.shape[:-1] + (1,))
    p = jax.nn.softmax(jnp.concatenate([s, sink], axis=-1), axis=-1)[..., :-1].astype(v.dtype)
    o = jnp.einsum('bnkgqs,bnskd->bnqkgd', p, v)
    return o.reshape(b, n, tq, h, d)


def even_mixer(h, conv_hist, k_hist, v_hist, start, w_in, conv_w, conv_b, ln_g, ln_b, qn_g, kn_g, sinks, w_out):
    b, t, _ = h.shape
    qd, kd = N_Q_HEADS * HEAD_DIM, N_KV_HEADS * HEAD_DIM
    a_val, a_gate, q, k, v = jnp.split(
        h @ w_in, [CONV_A_DIM, 2 * CONV_A_DIM, 2 * CONV_A_DIM + qd, 2 * CONV_A_DIM + qd + kd], axis=-1)
    u = a_val * jax.nn.sigmoid(a_gate)
    padded = jnp.concatenate([conv_hist, u], axis=1)
    c = jax.nn.silu(layernorm(causal_dwconv(padded, conv_w, conv_b), ln_g, ln_b))
    new_conv = padded[:, -(CONV_A_WIDTH - 1):]
    pos = start + jnp.arange(t, dtype=jnp.int32)
    q = partial_rope(rmsnorm(q.reshape(b, t, N_Q_HEADS, HEAD_DIM), qn_g), pos)
    k = partial_rope(rmsnorm(k.reshape(b, t, N_KV_HEADS, HEAD_DIM), kn_g), pos)
    v = v.reshape(b, t, N_KV_HEADS, HEAD_DIM)
    if k_hist is None:
        nblk = t // WINDOW
        qb = q.reshape(b, nblk, WINDOW, N_Q_HEADS, HEAD_DIM)
        kb = k.reshape(b, nblk, WINDOW, N_KV_HEADS, HEAD_DIM)
        vb = v.reshape(b, nblk, WINDOW, N_KV_HEADS, HEAD_DIM)
        kk = jnp.concatenate([jnp.concatenate([jnp.zeros_like(kb[:, :1]), kb[:, :-1]], axis=1), kb], axis=2)
        vv = jnp.concatenate([jnp.concatenate([jnp.zeros_like(vb[:, :1]), vb[:, :-1]], axis=1), vb], axis=2)
        q_pos = pos.reshape(nblk, WINDOW)
        k_pos = (jnp.arange(nblk, dtype=jnp.int32) * WINDOW + start - WINDOW)[:, None] + \
            jnp.arange(2 * WINDOW, dtype=jnp.int32)[None, :]
        o = sink_attention(qb, kk, vv, q_pos, k_pos, sinks)
        k_all, v_all = k, v
    else:
        wb = k_hist.shape[1]
        k_all = jnp.concatenate([k_hist, k], axis=1)
        v_all = jnp.concatenate([v_hist, v], axis=1)
        k_pos = start - wb + jnp.arange(wb + t, dtype=jnp.int32)
        o = sink_attention(q[:, None], k_all[:, None], v_all[:, None], pos[None], k_pos[None], sinks)
    o = o.reshape(b, t, N_Q_HEADS * HEAD_DIM)
    out = jnp.concatenate([c, o], axis=-1) @ w_out
    return out, new_conv, k_all[:, -WINDOW:], v_all[:, -WINDOW:]


def ssd_scan(x, dt, a, bm, cm, h0):
    b, t, h, p = x.shape
    g, n = bm.shape[2], bm.shape[3]
    r = h // g
    dty = x.dtype
    L = SSD_CHUNK if t % SSD_CHUNK == 0 else t
    c = t // L
    xc = x.reshape(b, c, L, g, r, p)
    bc = bm.reshape(b, c, L, g, n)
    cc = cm.reshape(b, c, L, g, n)
    dtc = dt.reshape(b, c, L, g, r)
    acum = jnp.cumsum(dtc * a.reshape(g, r), axis=2)
    causal = jnp.tril(jnp.ones((L, L), dtype=bool))[None, None, :, :, None, None]
    diff = acum[:, :, :, None] - acum[:, :, None, :]
    decay = jnp.exp(jnp.where(causal, diff, -jnp.inf))
    cb = jnp.einsum('bclgn,bcsgn->bclsg', cc, bc).astype(jnp.float32)
    w_intra = (cb[..., None] * decay * dtc[:, :, None]).astype(dty)
    y_diag = jnp.einsum('bclsgr,bcsgrp->bclgrp', w_intra, xc)
    decay_end = jnp.exp(acum[:, :, -1:] - acum)
    states = jnp.einsum('bclgn,bclgr,bclgrp->bcgrpn', bc, (decay_end * dtc).astype(dty), xc)
    chunk_decay = jnp.exp(acum[:, :, -1])

    def step(hc, inp):
        dec, st = inp
        return dec[..., None, None] * hc + st, hc

    h_final, h_prev = lax.scan(step, h0.reshape(b, g, r, p, n).astype(jnp.float32),
                               (jnp.moveaxis(chunk_decay, 1, 0), jnp.moveaxis(states.astype(jnp.float32), 1, 0)))
    h_prev = jnp.moveaxis(h_prev, 0, 1).astype(dty)
    y_off = jnp.einsum('bclgn,bcgrpn,bclgr->bclgrp', cc, h_prev, jnp.exp(acum).astype(dty))
    y = (y_diag + y_off).reshape(b, t, h, p)
    return y, h_final.reshape(b, h, p, n).astype(h0.dtype)


def odd_mixer(h, conv_hist, ssm_hist, w_in, conv_w, conv_b, dt_bias, a_log, d_skip, gn_g, w_out):
    b, t, _ = h.shape
    gn = SSM_GROUPS * SSM_STATE
    z, xbc, dt = jnp.split(h @ w_in, [D_INNER, D_INNER + SSM_CONV_DIM], axis=-1)
    padded = jnp.concatenate([conv_hist, xbc], axis=1)
    xbc = jax.nn.silu(causal_dwconv(padded, conv_w, conv_b))
    new_conv = padded[:, -(SSM_CONV_WIDTH - 1):]
    xs, bm, cm = jnp.split(xbc, [D_INNER, D_INNER + gn], axis=-1)
    xs = xs.reshape(b, t, SSM_HEADS, SSM_HEAD_DIM)
    bm = bm.reshape(b, t, SSM_GROUPS, SSM_STATE)
    cm = cm.reshape(b, t, SSM_GROUPS, SSM_STATE)
    dt = jax.nn.softplus(dt.astype(jnp.float32) + dt_bias.astype(jnp.float32))
    a = -jnp.exp(a_log.astype(jnp.float32))
    y, new_ssm = ssd_scan(xs, dt, a, bm, cm, ssm_hist)
    y = (y + d_skip[:, None] * xs).reshape(b, t, D_INNER) * jax.nn.silu(z)
    gs = D_INNER // SSM_GROUPS
    y = rmsnorm(y.reshape(b, t, SSM_GROUPS, gs), gn_g.reshape(SSM_GROUPS, gs)).reshape(b, t, D_INNER)
    return y @ w_out, new_conv, new_ssm


def conv_ffn(h, hist, w_gate, w_up, conv_w, conv_b, w_down):
    gate = h @ w_gate
    padded = jnp.concatenate([hist, gate], axis=1)
    gc = causal_dwconv(padded, conv_w, conv_b)
    y = (jax.nn.silu(gc) * (h @ w_up)) @ w_down
    return y, padded[:, -(FFN_CONV_WIDTH - 1):]


def setup_inputs(seed: int = 0) -> dict:
    key = jax.random.key(seed)
    ks = iter(jax.random.split(key, 48))

    def nrm(shape, scale):
        return scale * jax.random.normal(next(ks), shape, jnp.float32)

    def gain(shape):
        return 1.0 + nrm(shape, 0.02)

    dt0 = jnp.exp(jax.random.uniform(next(ks), (N_ODD, SSM_HEADS), jnp.float32,
                                     minval=math.log(1e-3), maxval=math.log(1e-1)))
    a_init = jax.random.uniform(next(ks), (N_ODD, SSM_HEADS), jnp.float32, minval=1.0, maxval=16.0)
    return {
        "x_prompt": nrm((BATCH, SEQ, D_MODEL), 1.0),
        "x_sample": nrm((DEC_BATCH, DEC_SEQ, D_MODEL), 1.0),
        "state_conv_a": nrm((N_EVEN, DEC_BATCH, CONV_A_WIDTH - 1, CONV_A_DIM), 0.5),
        "cache_win_k": nrm((N_EVEN, DEC_BATCH, WINDOW, N_KV_HEADS, HEAD_DIM), 1.0),
        "cache_win_v": nrm((N_EVEN, DEC_BATCH, WINDOW, N_KV_HEADS, HEAD_DIM), 1.0),
        "state_conv_c": nrm((N_ODD, DEC_BATCH, SSM_CONV_WIDTH - 1, SSM_CONV_DIM), 1.0),
        "state_ssm": nrm((N_ODD, DEC_BATCH, SSM_HEADS, SSM_HEAD_DIM, SSM_STATE), 0.1),
        "state_ffn_conv": nrm((DEPTH, DEC_BATCH, FFN_CONV_WIDTH - 1, D_FF), 1.0),
        "norm_mix_e": gain((N_EVEN, D_MODEL)),
        "w_in_e": nrm((N_EVEN, D_MODEL, IN_EVEN), D_MODEL ** -0.5),
        "conv_a_w": nrm((N_EVEN, CONV_A_WIDTH, CONV_A_DIM), CONV_A_WIDTH ** -0.5),
        "conv_a_b": nrm((N_EVEN, CONV_A_DIM), 0.01),
        "ln_a_g": gain((N_EVEN, CONV_A_DIM)),
        "ln_a_b": nrm((N_EVEN, CONV_A_DIM), 0.01),
        "q_norm_g": gain((N_EVEN, HEAD_DIM)),
        "k_norm_g": gain((N_EVEN, HEAD_DIM)),
        "sinks": nrm((N_EVEN, N_Q_HEADS), 1.0),
        "w_out_e": nrm((N_EVEN, OUT_EVEN_IN, D_MODEL), OUT_EVEN_IN ** -0.5),
        "norm_mix_o": gain((N_ODD, D_MODEL)),
        "w_in_o": nrm((N_ODD, D_MODEL, IN_ODD), D_MODEL ** -0.5),
        "conv_c_w": nrm((N_ODD, SSM_CONV_WIDTH, SSM_CONV_DIM), SSM_CONV_WIDTH ** -0.5),
        "conv_c_b": nrm((N_ODD, SSM_CONV_DIM), 0.01),
        "dt_bias": dt0 + jnp.log(-jnp.expm1(-dt0)),
        "a_log": jnp.log(a_init),
        "d_skip": gain((N_ODD, SSM_HEADS)),
        "gnorm_c": gain((N_ODD, D_INNER)),
        "w_out_o": nrm((N_ODD, D_INNER, D_MODEL), D_INNER ** -0.5),
        "norm_ffn": gain((DEPTH, D_MODEL)),
        "w_gate": nrm((DEPTH, D_MODEL, D_FF), D_MODEL ** -0.5),
        "w_up": nrm((DEPTH, D_MODEL, D_FF), D_MODEL ** -0.5),
        "ffn_conv_w": nrm((DEPTH, FFN_CONV_WIDTH, D_FF), FFN_CONV_WIDTH ** -0.5),
        "ffn_conv_b": nrm((DEPTH, D_FF), 0.01),
        "w_down": nrm((DEPTH, D_FF, D_MODEL), D_FF ** -0.5),
    }


def reference(x_prompt, x_sample, state_conv_a, cache_win_k, cache_win_v, state_conv_c, state_ssm, state_ffn_conv,
              norm_mix_e, w_in_e, conv_a_w, conv_a_b, ln_a_g, ln_a_b, q_norm_g, k_norm_g, sinks, w_out_e,
              norm_mix_o, w_in_o, conv_c_w, conv_c_b, dt_bias, a_log, d_skip, gnorm_c, w_out_o,
              norm_ffn, w_gate, w_up, ffn_conv_w, ffn_conv_b, w_down):
    xp, xs = x_prompt, x_sample
    bp = xp.shape[0]
    dty = xp.dtype
    ca_p, ca_s, wk_p, wk_s, wv_p, wv_s = [], [], [], [], [], []
    cc_p, cc_s, ss_p, ss_s, ff_p, ff_s = [], [], [], [], [], []
    for layer in range(DEPTH):
        i = layer // 2
        if layer % 2 == 0:
            ew = (w_in_e[i], conv_a_w[i], conv_a_b[i], ln_a_g[i], ln_a_b[i], q_norm_g[i], k_norm_g[i], sinks[i], w_out_e[i])
            dp, c1, k1, v1 = even_mixer(rmsnorm(xp, norm_mix_e[i]),
                                        jnp.zeros((bp, CONV_A_WIDTH - 1, CONV_A_DIM), dty), None, None, 0, *ew)
            ds, c2, k2, v2 = even_mixer(rmsnorm(xs, norm_mix_e[i]), state_conv_a[i], cache_win_k[i], cache_win_v[i],
                                        PAST_LEN, *ew)
            ca_p.append(c1); ca_s.append(c2); wk_p.append(k1); wk_s.append(k2); wv_p.append(v1); wv_s.append(v2)
        else:
            ow = (w_in_o[i], conv_c_w[i], conv_c_b[i], dt_bias[i], a_log[i], d_skip[i], gnorm_c[i], w_out_o[i])
            dp, c1, s1 = odd_mixer(rmsnorm(xp, norm_mix_o[i]),
                                   jnp.zeros((bp, SSM_CONV_WIDTH - 1, SSM_CONV_DIM), dty),
                                   jnp.zeros((bp, SSM_HEADS, SSM_HEAD_DIM, SSM_STATE), dty), *ow)
            ds, c2, s2 = odd_mixer(rmsnorm(xs, norm_mix_o[i]), state_conv_c[i], state_ssm[i], *ow)
            cc_p.append(c1); cc_s.append(c2); ss_p.append(s1); ss_s.append(s2)
        xp = xp + dp
        xs = xs + ds
        fw = (w_gate[layer], w_up[layer], ffn_conv_w[layer], ffn_conv_b[layer], w_down[layer])
        fp, f1 = conv_ffn(rmsnorm(xp, norm_ffn[layer]), jnp.zeros((bp, FFN_CONV_WIDTH - 1, D_FF), dty), *fw)
        fs, f2 = conv_ffn(rmsnorm(xs, norm_ffn[layer]), state_ffn_conv[layer], *fw)
        ff_p.append(f1); ff_s.append(f2)
        xp = xp + fp
        xs = xs + fs
    return (xp, xs,
            jnp.stack(ca_p), jnp.stack(ca_s),
            jnp.stack(wk_p), jnp.stack(wk_s),
            jnp.stack(wv_p), jnp.stack(wv_s),
            jnp.stack(cc_p), jnp.stack(cc_s),
            jnp.stack(ss_p), jnp.stack(ss_s),
            jnp.stack(ff_p), jnp.stack(ff_s))
```

```python
import functools

import jax
import jax.numpy as jnp
from jax import lax
from jax.experimental import pallas as pl
from jax.experimental.pallas import tpu as pltpu

F32 = jnp.float32
BF16 = jnp.bfloat16

D_MODEL = 1024
PAST_LEN = 8192
CONV_A_DIM = 512
CONV_A_WIDTH = 31
HEAD_DIM = 64
N_Q_HEADS = 8
N_KV_HEADS = 2
Q_PER_KV = N_Q_HEADS // N_KV_HEADS
QD = N_Q_HEADS * HEAD_DIM
KD = N_KV_HEADS * HEAD_DIM
WINDOW = 128
ROPE_DIM = 16
ROPE_THETA = 500000.0
D_INNER = 2048
SSM_HEAD_DIM = 64
SSM_HEADS = 32
SSM_STATE = 128
SSM_GROUPS = 4
HEADS_PER_GROUP = SSM_HEADS // SSM_GROUPS
GROUP_WIDTH = D_INNER // SSM_GROUPS
GN = SSM_GROUPS * SSM_STATE
SSM_CONV_WIDTH = 4
SSM_CONV_DIM = D_INNER + 2 * GN
SSD_CHUNK = 128
D_FF = 2816
FFN_CONV_WIDTH = 3
EPS = 1e-6

V7X_VMEM_LIMIT_BYTES = 58 * 1024 * 1024
SUBLANES = 8
LANES = 128


def _params(n_grid_dims):
    return pltpu.CompilerParams(
        dimension_semantics=("arbitrary",) * n_grid_dims,
        vmem_limit_bytes=V7X_VMEM_LIMIT_BYTES,
    )


def _resident():
    return pl.BlockSpec(memory_space=pltpu.VMEM)


def _rms(x, g_row):
    ms = jnp.mean(x * x, axis=-1, keepdims=True)
    return (x * lax.rsqrt(ms + EPS)) * g_row


def _sigmoid(x):
    return 1.0 / (1.0 + jnp.exp(-x))


def _silu(x):
    return x * _sigmoid(x)


def _softplus(x):
    return jnp.maximum(x, 0.0) + jnp.log(1.0 + jnp.exp(-jnp.abs(x)))


def _dot(a, b):
    return jnp.dot(a, b, preferred_element_type=F32)


def _dot_nt(a, b):
    return lax.dot_general(a, b, (((1,), (1,)), ((), ())), preferred_element_type=F32)


def _dot_tn(a, b):
    return lax.dot_general(a, b, (((0,), (0,)), ((), ())), preferred_element_type=F32)


def _dot_split(a, m_bf16):
    hi = a.astype(BF16)
    lo = (a - hi.astype(F32)).astype(BF16)
    return _dot(hi, m_bf16) + _dot(lo, m_bf16)


def _rope(x, ra, rb, rc):
    return x * ra + pltpu.roll(x, 8, 1) * rb + pltpu.roll(x, LANES - 8, 1) * rc


def _gate_norm_out(y, xs, z, x_res, dskip_row, gn_row, wout_ref):
    v = (y + dskip_row * xs) * _silu(z)
    parts = []
    for g in range(SSM_GROUPS):
        vg = v[:, g * GROUP_WIDTH:(g + 1) * GROUP_WIDTH]
        ms = jnp.mean(vg * vg, axis=-1, keepdims=True)
        parts.append(vg * lax.rsqrt(ms + EPS))
    vn = (jnp.concatenate(parts, axis=1) * gn_row).astype(BF16)
    return x_res + _dot(vn, wout_ref[...])


def _ffn_kernel(x_ref, hist_ref, g_ref, wg_ref, wu_ref, cw_ref, cb_ref, wd_ref,
                out_ref, hout_ref, gbuf, *, tm, hb, stride, cw):
    @pl.when(pl.program_id(1) == 0)
    def _():
        gbuf[0:hb, :] = hist_ref[...]

    x = x_ref[...]
    h = _rms(x, g_ref[...]).astype(BF16)
    acc = jnp.zeros((tm, D_MODEL), F32)
    for j in range(D_FF // cw):
        cs = slice(j * cw, (j + 1) * cw)
        gate = _dot(h, wg_ref[:, cs])
        up = _dot(h, wu_ref[:, cs])
        gbuf[hb:hb + tm, cs] = gate
        conv = (cw_ref[0:1, cs] * gbuf[hb - 2 * stride:hb - 2 * stride + tm, cs]
                + cw_ref[1:2, cs] * gbuf[hb - stride:hb - stride + tm, cs]
                + cw_ref[2:3, cs] * gate + cb_ref[:, cs])
        act = (_silu(conv) * up).astype(BF16)
        acc = acc + _dot(act, wd_ref[cs, :])
    out_ref[...] = x + acc
    tail = gbuf[tm:tm + hb, :]
    hout_ref[...] = tail
    gbuf[0:hb, :] = tail


def _ffn_call(x, hist, g, wg, wu, cwt, cbias, wd, *, nseq, tm, hb, stride):
    rows = x.shape[0]
    tiles = rows // (nseq * tm)
    kern = functools.partial(_ffn_kernel, tm=tm, hb=hb, stride=stride, cw=256)
    return pl.pallas_call(
        kern,
        grid=(nseq, tiles),
        in_specs=[
            pl.BlockSpec((tm, D_MODEL), lambda s, i: (s * tiles + i, 0)),
            pl.BlockSpec((hb, D_FF), lambda s, i: (s, 0)),
            _resident(), _resident(), _resident(), _resident(), _resident(), _resident(),
        ],
        out_specs=[
            pl.BlockSpec((tm, D_MODEL), lambda s, i: (s * tiles + i, 0)),
            pl.BlockSpec((hb, D_FF), lambda s, i: (s, 0)),
        ],
        out_shape=[
            jax.ShapeDtypeStruct((rows, D_MODEL), F32),
            jax.ShapeDtypeStruct((nseq * hb, D_FF), F32),
        ],
        scratch_shapes=[pltpu.VMEM((hb + tm, D_FF), F32)],
        compiler_params=_params(2),
        name="conv_ffn",
    )(x, hist, g, wg, wu, cwt, cbias, wd)


CONV_ROWS = 16


def _even_pre_kernel(x_ref, hist_ref, ra_ref, rb_ref, rc_ref, g_ref, win_ref, cw_ref, cb_ref,
                     lng_ref, lnb_ref, qg_ref, kg_ref, pmean_ref,
                     c_ref, q_ref, k_ref, v_ref, hout_ref, ubuf, cbuf, *, tm, hb, stride):
    @pl.when(pl.program_id(1) == 0)
    def _():
        ubuf[0:hb, :] = hist_ref[...]

    x = x_ref[...]
    h = _rms(x, g_ref[...]).astype(BF16)
    proj = _dot(h, win_ref[...])
    u = proj[:, 0:CONV_A_DIM] * _sigmoid(proj[:, CONV_A_DIM:2 * CONV_A_DIM])
    ubuf[hb:hb + tm, :] = u

    base = hb - (CONV_A_WIDTH - 1) * stride

    for r0 in range(0, tm, CONV_ROWS):
        acc = jnp.zeros((CONV_ROWS, CONV_A_DIM), F32)
        for j in range(CONV_A_WIDTH):
            off = r0 + base + j * stride
            acc = acc + cw_ref[j:j + 1, :] * ubuf[off:off + CONV_ROWS, :]
        cbuf[r0:r0 + CONV_ROWS, :] = acc + cb_ref[...]

    conv = cbuf[...]
    mu = jnp.mean(conv, axis=-1, keepdims=True)
    cen = conv - mu
    var = jnp.mean(cen * cen, axis=-1, keepdims=True)
    ln = cen * lax.rsqrt(var + EPS) * lng_ref[...] + lnb_ref[...]
    c_ref[...] = _silu(ln).astype(BF16)

    tail = ubuf[tm:tm + hb, :]
    hout_ref[...] = tail
    ubuf[0:hb, :] = tail

    ra, rb, rc = ra_ref[...], rb_ref[...], rc_ref[...]
    pmean = pmean_ref[...]
    q0 = 2 * CONV_A_DIM
    for s in range(QD // LANES):
        qs = proj[:, q0 + s * LANES:q0 + (s + 1) * LANES]
        ms = _dot_split(qs * qs, pmean)
        qn = qs * lax.rsqrt(ms + EPS) * qg_ref[...]
        q_ref[:, s * LANES:(s + 1) * LANES] = _rope(qn, ra, rb, rc).astype(BF16)
    ks = proj[:, q0 + QD:q0 + QD + KD]
    ms = _dot_split(ks * ks, pmean)
    kn = ks * lax.rsqrt(ms + EPS) * kg_ref[...]
    k_ref[...] = _rope(kn, ra, rb, rc)
    v_ref[...] = proj[:, q0 + QD + KD:q0 + QD + 2 * KD]


def _even_pre_call(x, hist, ra, rb, rc, g, win, cwt, cbias, lng, lnb, qg, kg, pmean,
                   *, nseq, tm, hb, stride):
    rows = x.shape[0]
    tiles = rows // (nseq * tm)
    rope_tiles = ra.shape[0] // tm
    kern = functools.partial(_even_pre_kernel, tm=tm, hb=hb, stride=stride)
    row_spec = lambda w: pl.BlockSpec((tm, w), lambda s, i: (s * tiles + i, 0))
    rope_spec = pl.BlockSpec((tm, LANES), lambda s, i: (i % rope_tiles, 0))
    return pl.pallas_call(
        kern,
        grid=(nseq, tiles),
        in_specs=[
            row_spec(D_MODEL),
            pl.BlockSpec((hb, CONV_A_DIM), lambda s, i: (s, 0)),
            rope_spec, rope_spec, rope_spec,
            _resident(), _resident(), _resident(), _resident(), _resident(), _resident(),
            _resident(), _resident(), _resident(),
        ],
        out_specs=[
            row_spec(CONV_A_DIM), row_spec(QD), row_spec(KD), row_spec(KD),
            pl.BlockSpec((hb, CONV_A_DIM), lambda s, i: (s, 0)),
        ],
        out_shape=[
            jax.ShapeDtypeStruct((rows, CONV_A_DIM), BF16),
            jax.ShapeDtypeStruct((rows, QD), BF16),
            jax.ShapeDtypeStruct((rows, KD), F32),
            jax.ShapeDtypeStruct((rows, KD), F32),
            jax.ShapeDtypeStruct((nseq * hb, CONV_A_DIM), F32),
        ],
        scratch_shapes=[pltpu.VMEM((hb + tm, CONV_A_DIM), F32), pltpu.VMEM((tm, CONV_A_DIM), F32)],
        compiler_params=_params(2),
        name="even_pre",
    )(x, hist, ra, rb, rc, g, win, cwt, cbias, lng, lnb, qg, kg, pmean)


def _softmax_with_sink(s, sink):
    m = jnp.maximum(jnp.max(s, axis=-1, keepdims=True), sink)
    p = jnp.exp(s - m)
    den = jnp.sum(p, axis=-1, keepdims=True) + jnp.exp(sink - m)
    return p / den


def _attn_out_kernel(sink_ref, q_ref, k_ref, v_ref, kp_ref, vp_ref, c_ref, x_ref, wout_ref,
                     out_ref, obuf, *, tq):
    first_tile = pl.program_id(1) == 0
    row = lax.broadcasted_iota(jnp.int32, (WINDOW, 2 * WINDOW), 0)
    col = lax.broadcasted_iota(jnp.int32, (WINDOW, 2 * WINDOW), 1)
    band = (col - row >= 1) & (col - row <= WINDOW)
    for qb in range(tq // WINDOW):
        rs = slice(qb * WINDOW, (qb + 1) * WINDOW)
        if qb == 0:
            kprev, vprev = kp_ref[...], vp_ref[...]
            valid = band & ((col >= WINDOW) | jnp.logical_not(first_tile))
        else:
            ps = slice((qb - 1) * WINDOW, qb * WINDOW)
            kprev, vprev = k_ref[ps, :], v_ref[ps, :]
            valid = band
        kk = jnp.concatenate([kprev, k_ref[rs, :]], axis=0).astype(BF16)
        vv = jnp.concatenate([vprev, v_ref[rs, :]], axis=0).astype(BF16)
        for hd in range(N_Q_HEADS):
            kv = hd // Q_PER_KV
            qh = q_ref[rs, hd * HEAD_DIM:(hd + 1) * HEAD_DIM]
            s = _dot_nt(qh, kk[:, kv * HEAD_DIM:(kv + 1) * HEAD_DIM]) * (HEAD_DIM ** -0.5)
            s = jnp.where(valid, s, -jnp.inf)
            p = _softmax_with_sink(s, sink_ref[hd]).astype(BF16)
            obuf[rs, hd * HEAD_DIM:(hd + 1) * HEAD_DIM] = _dot(p, vv[:, kv * HEAD_DIM:(kv + 1) * HEAD_DIM])
    mixed = _dot(c_ref[...], wout_ref[0:CONV_A_DIM, :]) + _dot(obuf[...].astype(BF16), wout_ref[CONV_A_DIM:, :])
    out_ref[...] = x_ref[...] + mixed


def _attn_out_call(sinks, q, k, v, c, x, wout, *, nseq, tq):
    rows = x.shape[0]
    tiles = rows // (nseq * tq)
    bpt = tq // WINDOW
    kern = functools.partial(_attn_out_kernel, tq=tq)
    row_spec = lambda w: pl.BlockSpec((tq, w), lambda s, i: (s * tiles + i, 0))
    prev_spec = pl.BlockSpec((WINDOW, KD), lambda s, i: (jnp.maximum((s * tiles + i) * bpt - 1, 0), 0))
    return pl.pallas_call(
        kern,
        grid=(nseq, tiles),
        in_specs=[
            pl.BlockSpec(memory_space=pltpu.SMEM),
            row_spec(QD), row_spec(KD), row_spec(KD), prev_spec, prev_spec,
            row_spec(CONV_A_DIM), row_spec(D_MODEL), _resident(),
        ],
        out_specs=row_spec(D_MODEL),
        out_shape=jax.ShapeDtypeStruct((rows, D_MODEL), F32),
        scratch_shapes=[pltpu.VMEM((tq, QD), F32)],
        compiler_params=_params(2),
        name="attn_out",
    )(sinks, q, k, v, k, v, c, x, wout)


SAMPLE_SEQ_BLOCK = 8


def _sample_attn_kernel(sink_ref, q_ref, kn_ref, vn_ref, kh_ref, vh_ref, o_ref, *, t):
    rows = Q_PER_KV * t
    keys = WINDOW + t
    r = lax.broadcasted_iota(jnp.int32, (rows, keys), 0)
    j = lax.broadcasted_iota(jnp.int32, (rows, keys), 1)
    tq = r % t
    valid = (j > tq) & (j <= tq + WINDOW)
    grp = lax.broadcasted_iota(jnp.int32, (rows, 1), 0) // t
    for b in range(SAMPLE_SEQ_BLOCK):
        kall = jnp.concatenate([kh_ref[b], kn_ref[b]], axis=0).astype(BF16)
        vall = jnp.concatenate([vh_ref[b], vn_ref[b]], axis=0).astype(BF16)
        for kv in range(N_KV_HEADS):
            sink = jnp.zeros((rows, 1), F32)
            for g in range(Q_PER_KV):
                sink = jnp.where(grp == g, sink_ref[kv * Q_PER_KV + g], sink)
            ls = slice(kv * HEAD_DIM, (kv + 1) * HEAD_DIM)
            s = _dot_nt(q_ref[b, kv], kall[:, ls]) * (HEAD_DIM ** -0.5)
            s = jnp.where(valid, s, -jnp.inf)
            p = _softmax_with_sink(s, sink).astype(BF16)
            o_ref[b, kv] = _dot(p, vall[:, ls])


def _sample_attn_call(sinks, q, kn, vn, kh, vh):
    nb, _, rows, _ = q.shape
    t = kn.shape[1]
    bb = SAMPLE_SEQ_BLOCK
    kern = functools.partial(_sample_attn_kernel, t=t)
    return pl.pallas_call(
        kern,
        grid=(nb // bb,),
        in_specs=[
            pl.BlockSpec(memory_space=pltpu.SMEM),
            pl.BlockSpec((bb, N_KV_HEADS, rows, HEAD_DIM), lambda i: (i, 0, 0, 0)),
            pl.BlockSpec((bb, t, KD), lambda i: (i, 0, 0)),
            pl.BlockSpec((bb, t, KD), lambda i: (i, 0, 0)),
            pl.BlockSpec((bb, WINDOW, KD), lambda i: (i, 0, 0)),
            pl.BlockSpec((bb, WINDOW, KD), lambda i: (i, 0, 0)),
        ],
        out_specs=pl.BlockSpec((bb, N_KV_HEADS, rows, HEAD_DIM), lambda i: (i, 0, 0, 0)),
        out_shape=jax.ShapeDtypeStruct((nb, N_KV_HEADS, rows, HEAD_DIM), F32),
        compiler_params=_params(1),
        name="sample_attn",
    )(sinks, q, kn, vn, kh, vh)


def _even_out_kernel(c_ref, o_ref, x_ref, wout_ref, out_ref):
    mixed = _dot(c_ref[...], wout_ref[0:CONV_A_DIM, :]) + _dot(o_ref[...].astype(BF16), wout_ref[CONV_A_DIM:, :])
    out_ref[...] = x_ref[...] + mixed


def _even_out_call(c, o, x, wout, *, tm):
    rows = x.shape[0]
    row_spec = lambda w: pl.BlockSpec((tm, w), lambda i: (i, 0))
    return pl.pallas_call(
        _even_out_kernel,
        grid=(rows // tm,),
        in_specs=[row_spec(CONV_A_DIM), row_spec(QD), row_spec(D_MODEL), _resident()],
        out_specs=row_spec(D_MODEL),
        out_shape=jax.ShapeDtypeStruct((rows, D_MODEL), F32),
        compiler_params=_params(1),
        name="even_out",
    )(c, o, x, wout)


def _odd_pre_kernel(x_ref, hist_ref, g_ref, wz_ref, wx_ref, wdt_ref, wdtt_ref, cw_ref, cb_ref, dtb_ref, dtbt_ref,
                    z_ref, xs_ref, b_ref, c_ref, dt_ref, dtt_ref, hout_ref, xbuf, *, tm, hb, stride, cw):
    @pl.when(pl.program_id(1) == 0)
    def _():
        xbuf[0:hb, :] = hist_ref[...]

    h = _rms(x_ref[...], g_ref[...]).astype(BF16)
    for j in range(D_INNER // cw):
        cs = slice(j * cw, (j + 1) * cw)
        z_ref[:, cs] = _dot(h, wz_ref[:, cs])
    for j in range(SSM_CONV_DIM // cw):
        cs = slice(j * cw, (j + 1) * cw)
        pre = _dot(h, wx_ref[:, cs])
        xbuf[hb:hb + tm, cs] = pre
        conv = cw_ref[3:4, cs] * pre + cb_ref[:, cs]
        for tap in range(SSM_CONV_WIDTH - 1):
            off = hb - (SSM_CONV_WIDTH - 1 - tap) * stride
            conv = conv + cw_ref[tap:tap + 1, cs] * xbuf[off:off + tm, cs]
        act = _silu(conv)
        if j * cw < D_INNER:
            xs_ref[:, cs] = act
        elif j * cw < D_INNER + GN:
            b_ref[:, j * cw - D_INNER:(j + 1) * cw - D_INNER] = act
        else:
            c_ref[:, j * cw - D_INNER - GN:(j + 1) * cw - D_INNER - GN] = act
    dt_ref[...] = _softplus(_dot(h, wdt_ref[...]) + dtb_ref[...])
    dtt_ref[...] = _softplus(_dot_nt(wdtt_ref[...], h) + dtbt_ref[...])
    tail = xbuf[tm:tm + hb, :]
    hout_ref[...] = tail
    xbuf[0:hb, :] = tail


def _odd_pre_call(x, hist, g, wz, wx, wdt, wdtt, cwt, cbias, dtb, dtbt, *, nseq, tm, hb, stride):
    rows = x.shape[0]
    tiles = rows // (nseq * tm)
    kern = functools.partial(_odd_pre_kernel, tm=tm, hb=hb, stride=stride, cw=512)
    row_spec = lambda w: pl.BlockSpec((tm, w), lambda s, i: (s * tiles + i, 0))
    return pl.pallas_call(
        kern,
        grid=(nseq, tiles),
        in_specs=[
            row_spec(D_MODEL),
            pl.BlockSpec((hb, SSM_CONV_DIM), lambda s, i: (s, 0)),
        ] + [_resident()] * 9,
        out_specs=[
            row_spec(D_INNER), row_spec(D_INNER), row_spec(GN), row_spec(GN), row_spec(SSM_HEADS),
            pl.BlockSpec((SSM_HEADS, tm), lambda s, i: (0, s * tiles + i)),
            pl.BlockSpec((hb, SSM_CONV_DIM), lambda s, i: (s, 0)),
        ],
        out_shape=[
            jax.ShapeDtypeStruct((rows, D_INNER), F32),
            jax.ShapeDtypeStruct((rows, D_INNER), F32),
            jax.ShapeDtypeStruct((rows, GN), F32),
            jax.ShapeDtypeStruct((rows, GN), F32),
            jax.ShapeDtypeStruct((rows, SSM_HEADS), F32),
            jax.ShapeDtypeStruct((SSM_HEADS, rows), F32),
            jax.ShapeDtypeStruct((nseq * hb, SSM_CONV_DIM), F32),
        ],
        scratch_shapes=[pltpu.VMEM((hb + tm, SSM_CONV_DIM), F32)],
        compiler_params=_params(2),
        name="odd_pre",
    )(x, hist, g, wz, wx, wdt, wdtt, cwt, cbias, dtb, dtbt)


def _ssd_kernel(xs_ref, b_ref, c_ref, dt_ref, dtt_ref, z_ref, x_ref, alog_ref, alogt_ref, dskip_ref, gn_ref,
                wout_ref, tril_ref, triu_ref, expand_ref,
                out_ref, state_ref, ht, ybuf):
    L = SSD_CHUNK
    N = SSM_STATE
    chunk = pl.program_id(1)

    @pl.when(chunk == 0)
    def _():
        ht[...] = jnp.zeros_like(ht)

    xs = xs_ref[...]
    dt = dt_ref[...]
    dtt = dtt_ref[...]
    a_row = -jnp.exp(alog_ref[...])
    a_col = -jnp.exp(alogt_ref[...])
    acum = jnp.dot(tril_ref[...], dt * a_row, preferred_element_type=F32, precision=lax.Precision.HIGHEST)
    acumt = jnp.dot(dtt * a_col, triu_ref[...], preferred_element_type=F32, precision=lax.Precision.HIGHEST)
    last = acum[L - 1:L, :]

    row = lax.broadcasted_iota(jnp.int32, (L, L), 0)
    col = lax.broadcasted_iota(jnp.int32, (L, L), 1)
    causal = row >= col
    lane = lax.broadcasted_iota(jnp.int32, (L, 2 * SSM_HEAD_DIM), 1)
    low_half = lane < SSM_HEAD_DIM

    cbs = []
    for g in range(SSM_GROUPS):
        gs = slice(g * N, (g + 1) * N)
        cbs.append(_dot_nt(c_ref[:, gs].astype(BF16), b_ref[:, gs].astype(BF16)))

    for pair in range(SSM_HEADS // 2):
        ws = []
        for hd in (2 * pair, 2 * pair + 1):
            diff = jnp.where(causal, acum[:, hd:hd + 1] - acumt[hd:hd + 1, :], -jnp.inf)
            ws.append(cbs[hd // HEADS_PER_GROUP] * jnp.exp(diff) * dtt[hd:hd + 1, :])
        wcat = jnp.concatenate(ws, axis=1).astype(BF16)
        ls = slice(pair * 2 * SSM_HEAD_DIM, (pair + 1) * 2 * SSM_HEAD_DIM)
        xpair = xs[:, ls]
        xbd = jnp.concatenate([jnp.where(low_half, xpair, 0.0), jnp.where(low_half, 0.0, xpair)], axis=0)
        ybuf[:, ls] = _dot(wcat, xbd.astype(BF16))

    expand = expand_ref[...]
    ea = _dot_split(jnp.exp(acum), expand)
    de = _dot_split(jnp.exp(last - acum) * dt, expand)
    cd = _dot_split(jnp.broadcast_to(jnp.exp(last), (SUBLANES, SSM_HEADS)), expand)[0:1, :]
    xw = (xs * de).astype(BF16)

    y_parts = []
    for g in range(SSM_GROUPS):
        gs = slice(g * N, (g + 1) * N)
        ws_ = slice(g * GROUP_WIDTH, (g + 1) * GROUP_WIDTH)
        hg = ht[gs, :]
        y_parts.append(_dot(c_ref[:, gs].astype(BF16), hg.astype(BF16)))
        new = _dot(b_ref[:, gs].T.astype(BF16), xw[:, ws_])
        ht[gs, :] = cd[:, ws_] * hg + new
    y = ybuf[...] + ea * jnp.concatenate(y_parts, axis=1)

    out_ref[...] = _gate_norm_out(y, xs, z_ref[...], x_ref[...], dskip_ref[...], gn_ref[...], wout_ref)

    @pl.when(chunk == pl.num_programs(1) - 1)
    def _():
        for g in range(SSM_GROUPS):
            state_ref[0, g * GROUP_WIDTH:(g + 1) * GROUP_WIDTH, :] = ht[g * N:(g + 1) * N, :].T


def _ssd_call(xs, bm, cm, dt, dtt, z, x, alog, alogt, dskip, gn, wout, tril, triu, expand, *, nseq):
    rows = x.shape[0]
    L = SSD_CHUNK
    chunks = rows // (nseq * L)
    row_spec = lambda w: pl.BlockSpec((L, w), lambda s, i: (s * chunks + i, 0))
    return pl.pallas_call(
        _ssd_kernel,
        grid=(nseq, chunks),
        in_specs=[
            row_spec(D_INNER), row_spec(GN), row_spec(GN), row_spec(SSM_HEADS),
            pl.BlockSpec((SSM_HEADS, L), lambda s, i: (0, s * chunks + i)),
            row_spec(D_INNER), row_spec(D_MODEL),
        ] + [_resident()] * 8,
        out_specs=[
            row_spec(D_MODEL),
            pl.BlockSpec((1, D_INNER, SSM_STATE), lambda s, i: (s, 0, 0)),
        ],
        out_shape=[
            jax.ShapeDtypeStruct((rows, D_MODEL), F32),
            jax.ShapeDtypeStruct((nseq, D_INNER, SSM_STATE), F32),
        ],
        scratch_shapes=[pltpu.VMEM((GN, GROUP_WIDTH), F32), pltpu.VMEM((L, D_INNER), F32)],
        compiler_params=_params(2),
        name="ssd_chunk",
    )(xs, bm, cm, dt, dtt, z, x, alog, alogt, dskip, gn, wout, tril, triu, expand)


def _sample_ssd_intra_kernel(xs_ref, b_ref, c_ref, dt_ref, alog_ref, expand_ref, gexp_ref,
                             ydiag_ref, xw_ref, ea_ref, cd_ref, *, t, nb):
    a_row = -jnp.exp(alog_ref[...])
    expand = expand_ref[...]
    gexp = gexp_ref[...]
    blk = lambda l: slice(l * nb, (l + 1) * nb)
    dts = [dt_ref[blk(l), :] for l in range(t)]
    acum = []
    run = None
    for l in range(t):
        step = dts[l] * a_row
        run = step if run is None else run + step
        acum.append(run)
    last = acum[t - 1]
    cd_ref[...] = _dot_split(jnp.exp(last), expand)
    for l in range(t):
        ea_ref[blk(l), :] = _dot_split(jnp.exp(acum[l]), expand)
        xw_ref[blk(l), :] = xs_ref[blk(l), :] * _dot_split(jnp.exp(last - acum[l]) * dts[l], expand)
        yd = jnp.zeros((nb, D_INNER), F32)
        for s in range(l + 1):
            cbx = _dot((c_ref[blk(l), :] * b_ref[blk(s), :]).astype(BF16), gexp)
            w = _dot_split(jnp.exp(acum[l] - acum[s]) * dts[s], expand)
            yd = yd + cbx * w * xs_ref[blk(s), :]
        ydiag_ref[blk(l), :] = yd


def _sample_ssd_intra_call(xs, bm, cm, dt, alog, expand, gexp, *, t, nb):
    rows = xs.shape[0]
    kern = functools.partial(_sample_ssd_intra_kernel, t=t, nb=nb)
    return pl.pallas_call(
        kern,
        in_specs=[_resident()] * 7,
        out_specs=[_resident()] * 4,
        out_shape=[
            jax.ShapeDtypeStruct((rows, D_INNER), F32),
            jax.ShapeDtypeStruct((rows, D_INNER), F32),
            jax.ShapeDtypeStruct((rows, D_INNER), F32),
            jax.ShapeDtypeStruct((nb, D_INNER), F32),
        ],
        compiler_params=pltpu.CompilerParams(vmem_limit_bytes=V7X_VMEM_LIMIT_BYTES),
        name="sample_ssd_intra",
    )(xs, bm, cm, dt, alog, expand, gexp)


def _sample_ssd_state_kernel(h0_ref, c_ref, b_ref, xw_ref, ydiag_ref, ea_ref, cd_ref, y_ref, hnew_ref):
    N = SSM_STATE
    for j in range(SAMPLE_SEQ_BLOCK):
        ht = h0_ref[j].T
        cb_ = c_ref[:, j, :].astype(BF16)
        bb_ = b_ref[:, j, :].astype(BF16)
        xw = xw_ref[:, j, :].astype(BF16)
        y_parts, new_parts = [], []
        for g in range(SSM_GROUPS):
            gs = slice(g * N, (g + 1) * N)
            ws_ = slice(g * GROUP_WIDTH, (g + 1) * GROUP_WIDTH)
            y_parts.append(_dot(cb_[:, gs], ht[:, ws_].astype(BF16)))
            new_parts.append(_dot_tn(bb_[:, gs], xw[:, ws_]))
        y_ref[:, j, :] = ydiag_ref[:, j, :] + ea_ref[:, j, :] * jnp.concatenate(y_parts, axis=1)
        hnew = cd_ref[j:j + 1, :] * ht + jnp.concatenate(new_parts, axis=1)
        hnew_ref[j] = hnew.T


def _sample_ssd_state_call(h0, cm, bm, xw, ydiag, ea, cd):
    nb = h0.shape[0]
    t = cm.shape[0]
    bb = SAMPLE_SEQ_BLOCK
    tspec = lambda w: pl.BlockSpec((t, bb, w), lambda i: (0, i, 0))
    return pl.pallas_call(
        _sample_ssd_state_kernel,
        grid=(nb // bb,),
        in_specs=[
            pl.BlockSpec((bb, D_INNER, SSM_STATE), lambda i: (i, 0, 0)),
            tspec(GN), tspec(GN), tspec(D_INNER), tspec(D_INNER), tspec(D_INNER),
            pl.BlockSpec((bb, D_INNER), lambda i: (i, 0)),
        ],
        out_specs=[
            tspec(D_INNER),
            pl.BlockSpec((bb, D_INNER, SSM_STATE), lambda i: (i, 0, 0)),
        ],
        out_shape=[
            jax.ShapeDtypeStruct((t, nb, D_INNER), F32),
            jax.ShapeDtypeStruct((nb, D_INNER, SSM_STATE), F32),
        ],
        compiler_params=_params(1),
        name="sample_ssd_state",
    )(h0, cm, bm, xw, ydiag, ea, cd)


def _odd_out_kernel(y_ref, xs_ref, z_ref, x_ref, dskip_ref, gn_ref, wout_ref, out_ref):
    out_ref[...] = _gate_norm_out(y_ref[...], xs_ref[...], z_ref[...], x_ref[...], dskip_ref[...], gn_ref[...],
                                  wout_ref)


def _odd_out_call(y, xs, z, x, dskip, gn, wout, *, tm):
    rows = x.shape[0]
    row_spec = lambda w: pl.BlockSpec((tm, w), lambda i: (i, 0))
    return pl.pallas_call(
        _odd_out_kernel,
        grid=(rows // tm,),
        in_specs=[row_spec(D_INNER), row_spec(D_INNER), row_spec(D_INNER), row_spec(D_MODEL),
                  _resident(), _resident(), _resident()],
        out_specs=row_spec(D_MODEL),
        out_shape=jax.ShapeDtypeStruct((rows, D_MODEL), F32),
        compiler_params=_params(1),
        name="odd_out",
    )(y, xs, z, x, dskip, gn, wout)


def _rope_tables(pos):
    half = ROPE_DIM // 2
    inv_freq = ROPE_THETA ** (-jnp.arange(0, ROPE_DIM, 2, dtype=F32) / ROPE_DIM)
    ang = pos.astype(F32)[:, None] * inv_freq[None, :]
    cos, sin = jnp.cos(ang), jnp.sin(ang)
    n = pos.shape[0]
    rest = HEAD_DIM - ROPE_DIM
    ra = jnp.concatenate([cos, cos, jnp.ones((n, rest), F32)], axis=1)
    rb = jnp.concatenate([jnp.zeros((n, half), F32), sin, jnp.zeros((n, rest), F32)], axis=1)
    rc = jnp.concatenate([-sin, jnp.zeros((n, half + rest), F32)], axis=1)
    rep = LANES // HEAD_DIM
    return tuple(jnp.tile(m, (1, rep)) for m in (ra, rb, rc))


def _head_mean_matrix():
    idx = jnp.arange(LANES) // HEAD_DIM
    return ((idx[:, None] == idx[None, :]).astype(F32) / HEAD_DIM).astype(BF16)


def _head_expand_matrix():
    idx = jnp.arange(D_INNER) // SSM_HEAD_DIM
    return (jnp.arange(SSM_HEADS)[:, None] == idx[None, :]).astype(BF16)


def _group_expand_matrix():
    src = jnp.arange(GN) // SSM_STATE
    dst = jnp.arange(D_INNER) // GROUP_WIDTH
    return (src[:, None] == dst[None, :]).astype(BF16)


PROMPT_TILE = 512
SAMPLE_TILE = 512


def _row(v):
    return v.reshape(1, -1)


def _time_major_hist(state):
    return jnp.swapaxes(state, 0, 1).reshape(-1, state.shape[-1])


def _batch_major_hist(rows, nb):
    return jnp.swapaxes(rows.reshape(-1, nb, rows.shape[-1]), 0, 1)


def _even_layer(xp, xs, conv_hist_s, k_hist, v_hist, norm_g, w_in, conv_w, conv_b, ln_g, ln_b, qn_g, kn_g, sinks,
                w_out, tables, *, bp, tp, nb, ts):
    g = _row(norm_g)
    win = w_in.astype(BF16)
    wout = w_out.astype(BF16)
    cwt = jnp.pad(conv_w, ((0, 1), (0, 0)))
    shared = (g, win, cwt, _row(conv_b), _row(ln_g), _row(ln_b),
              _row(jnp.tile(qn_g, LANES // HEAD_DIM)), _row(jnp.tile(kn_g, LANES // HEAD_DIM)), tables["pmean"])

    hb = 32
    c, q, k, v, hout = _even_pre_call(xp, jnp.zeros((bp * hb, CONV_A_DIM), F32), *tables["rope_p"], *shared,
                                      nseq=bp, tm=PROMPT_TILE, hb=hb, stride=1)
    xp_new = _attn_out_call(sinks, q, k, v, c, xp, wout, nseq=bp, tq=PROMPT_TILE)
    conv_p = hout.reshape(bp, hb, CONV_A_DIM)[:, hb - (CONV_A_WIDTH - 1):]
    wk_p = k.reshape(bp, tp, N_KV_HEADS, HEAD_DIM)[:, -WINDOW:]
    wv_p = v.reshape(bp, tp, N_KV_HEADS, HEAD_DIM)[:, -WINDOW:]

    hb_s = (CONV_A_WIDTH - 1) * nb
    c, q, k, v, hout = _even_pre_call(xs, _time_major_hist(conv_hist_s), *tables["rope_s"], *shared,
                                      nseq=1, tm=SAMPLE_TILE, hb=hb_s, stride=nb)
    conv_s = _batch_major_hist(hout, nb)
    k_new = jnp.swapaxes(k.reshape(ts, nb, KD), 0, 1)
    v_new = jnp.swapaxes(v.reshape(ts, nb, KD), 0, 1)
    q_b = q.reshape(ts, nb, N_KV_HEADS, Q_PER_KV, HEAD_DIM).transpose(1, 2, 3, 0, 4)
    q_b = q_b.reshape(nb, N_KV_HEADS, Q_PER_KV * ts, HEAD_DIM)
    kh = k_hist.reshape(nb, WINDOW, KD)
    vh = v_hist.reshape(nb, WINDOW, KD)
    o = _sample_attn_call(sinks, q_b, k_new, v_new, kh, vh)
    o = o.reshape(nb, N_KV_HEADS, Q_PER_KV, ts, HEAD_DIM).transpose(3, 0, 1, 2, 4).reshape(ts * nb, QD)
    xs_new = _even_out_call(c, o, xs, wout, tm=SAMPLE_TILE)
    wk_s = jnp.concatenate([kh, k_new], axis=1)[:, -WINDOW:].reshape(nb, WINDOW, N_KV_HEADS, HEAD_DIM)
    wv_s = jnp.concatenate([vh, v_new], axis=1)[:, -WINDOW:].reshape(nb, WINDOW, N_KV_HEADS, HEAD_DIM)
    return xp_new, xs_new, conv_p, conv_s, wk_p, wk_s, wv_p, wv_s


def _odd_layer(xp, xs, conv_hist_s, ssm_hist_s, norm_g, w_in, conv_w, conv_b, dt_bias, a_log, d_skip, gn_g, w_out,
               tables, *, bp, tp, nb, ts):
    g = _row(norm_g)
    wz = w_in[:, :D_INNER].astype(BF16)
    wx = w_in[:, D_INNER:D_INNER + SSM_CONV_DIM].astype(BF16)
    wdt = w_in[:, D_INNER + SSM_CONV_DIM:].astype(BF16)
    wout = w_out.astype(BF16)
    dskip = _row(jnp.repeat(d_skip, SSM_HEAD_DIM))
    gn = _row(gn_g)
    alog = _row(a_log)
    pre_shared = (g, wz, wx, wdt, wdt.T, conv_w, _row(conv_b), _row(dt_bias), dt_bias.reshape(-1, 1))
    keep = SSM_CONV_WIDTH - 1

    hb = SUBLANES
    z, xc, bm, cm, dt, dtt, hout = _odd_pre_call(xp, jnp.zeros((bp * hb, SSM_CONV_DIM), F32), *pre_shared,
                                                 nseq=bp, tm=PROMPT_TILE, hb=hb, stride=1)
    xp_new, state_p = _ssd_call(xc, bm, cm, dt, dtt, z, xp, alog, a_log.reshape(-1, 1), dskip, gn, wout,
                                tables["tril"], tables["triu"], tables["expand"], nseq=bp)
    conv_p = hout.reshape(bp, hb, SSM_CONV_DIM)[:, hb - keep:]
    ssm_p = state_p.reshape(bp, SSM_HEADS, SSM_HEAD_DIM, SSM_STATE)

    hb_s = keep * nb
    z, xc, bm, cm, dt, _, hout = _odd_pre_call(xs, _time_major_hist(conv_hist_s), *pre_shared,
                                               nseq=1, tm=SAMPLE_TILE, hb=hb_s, stride=nb)
    conv_s = _batch_major_hist(hout, nb)
    ydiag, xw, ea, cd = _sample_ssd_intra_call(xc, bm, cm, dt, alog, tables["expand"], tables["gexp"], t=ts, nb=nb)
    tm3 = lambda a: a.reshape(ts, nb, a.shape[-1])
    y, state_s = _sample_ssd_state_call(ssm_hist_s.reshape(nb, D_INNER, SSM_STATE), tm3(cm), tm3(bm), tm3(xw),
                                        tm3(ydiag), tm3(ea), cd)
    xs_new = _odd_out_call(y.reshape(ts * nb, D_INNER), xc, z, xs, dskip, gn, wout, tm=SAMPLE_TILE)
    ssm_s = state_s.reshape(nb, SSM_HEADS, SSM_HEAD_DIM, SSM_STATE)
    return xp_new, xs_new, conv_p, conv_s, ssm_p, ssm_s


def _ffn_layer(xp, xs, hist_s, norm_g, w_gate, w_up, conv_w, conv_b, w_down, *, bp, nb):
    keep = FFN_CONV_WIDTH - 1
    shared = (_row(norm_g), w_gate.astype(BF16), w_up.astype(BF16), conv_w, _row(conv_b), w_down.astype(BF16))
    hb = SUBLANES
    xp_new, hout = _ffn_call(xp, jnp.zeros((bp * hb, D_FF), F32), *shared, nseq=bp, tm=PROMPT_TILE, hb=hb, stride=1)
    ffn_p = hout.reshape(bp, hb, D_FF)[:, hb - keep:]
    xs_new, hout = _ffn_call(xs, _time_major_hist(hist_s), *shared, nseq=1, tm=SAMPLE_TILE, hb=keep * nb, stride=nb)
    ffn_s = _batch_major_hist(hout, nb)
    return xp_new, xs_new, ffn_p, ffn_s


def kernel(x_prompt, x_sample, state_conv_a, cache_win_k, cache_win_v, state_conv_c, state_ssm, state_ffn_conv,
           norm_mix_e, w_in_e, conv_a_w, conv_a_b, ln_a_g, ln_a_b, q_norm_g, k_norm_g, sinks, w_out_e,
           norm_mix_o, w_in_o, conv_c_w, conv_c_b, dt_bias, a_log, d_skip, gnorm_c, w_out_o,
           norm_ffn, w_gate, w_up, ffn_conv_w, ffn_conv_b, w_down):
    bp, tp, _ = x_prompt.shape
    nb, ts, _ = x_sample.shape
    depth = norm_ffn.shape[0]

    pos_s = PAST_LEN + jnp.repeat(jnp.arange(ts, dtype=jnp.int32), nb)
    tril = jnp.tril(jnp.ones((SSD_CHUNK, SSD_CHUNK), F32))
    tables = {
        "rope_p": _rope_tables(jnp.arange(tp, dtype=jnp.int32)),
        "rope_s": _rope_tables(pos_s),
        "pmean": _head_mean_matrix(),
        "expand": _head_expand_matrix(),
        "gexp": _group_expand_matrix(),
        "tril": tril,
        "triu": tril.T,
    }
    dims = dict(bp=bp, tp=tp, nb=nb, ts=ts)

    xp = x_prompt.reshape(bp * tp, D_MODEL)
    xs = jnp.swapaxes(x_sample, 0, 1).reshape(ts * nb, D_MODEL)
    ca_p, ca_s, wk_p, wk_s, wv_p, wv_s = [], [], [], [], [], []
    cc_p, cc_s, ss_p, ss_s, ff_p, ff_s = [], [], [], [], [], []
    for layer in range(depth):
        i = layer // 2
        if layer % 2 == 0:
            xp, xs, c1, c2, k1, k2, v1, v2 = _even_layer(
                xp, xs, state_conv_a[i], cache_win_k[i], cache_win_v[i], norm_mix_e[i], w_in_e[i], conv_a_w[i],
                conv_a_b[i], ln_a_g[i], ln_a_b[i], q_norm_g[i], k_norm_g[i], sinks[i], w_out_e[i], tables, **dims)
            ca_p.append(c1); ca_s.append(c2); wk_p.append(k1); wk_s.append(k2); wv_p.append(v1); wv_s.append(v2)
        else:
            xp, xs, c1, c2, s1, s2 = _odd_layer(
                xp, xs, state_conv_c[i], state_ssm[i], norm_mix_o[i], w_in_o[i], conv_c_w[i], conv_c_b[i],
                dt_bias[i], a_log[i], d_skip[i], gnorm_c[i], w_out_o[i], tables, **dims)
            cc_p.append(c1); cc_s.append(c2); ss_p.append(s1); ss_s.append(s2)
        xp, xs, f1, f2 = _ffn_layer(xp, xs, state_ffn_conv[layer], norm_ffn[layer], w_gate[layer], w_up[layer],
                                    ffn_conv_w[layer], ffn_conv_b[layer], w_down[layer], bp=bp, nb=nb)
        ff_p.append(f1); ff_s.append(f2)

    y_prompt = xp.reshape(bp, tp, D_MODEL)
    y_sample = jnp.swapaxes(xs.reshape(ts, nb, D_MODEL), 0, 1)
    return (y_prompt, y_sample,
            jnp.stack(ca_p), jnp.stack(ca_s),
            jnp.stack(wk_p), jnp.stack(wk_s),
            jnp.stack(wv_p), jnp.stack(wv_s),
            jnp.stack(cc_p), jnp.stack(cc_s),
            jnp.stack(ss_p), jnp.stack(ss_s),
            jnp.stack(ff_p), jnp.stack(ff_s))
```

```python
import functools

import jax
import jax.numpy as jnp
from jax import lax
from jax.experimental import pallas as pl
from jax.experimental.pallas import tpu as pltpu

F32 = jnp.float32
BF16 = jnp.bfloat16

D_MODEL = 1024
PAST_LEN = 8192
CONV_A_DIM = 512
CONV_A_WIDTH = 31
HEAD_DIM = 64
N_Q_HEADS = 8
N_KV_HEADS = 2
Q_PER_KV = N_Q_HEADS // N_KV_HEADS
QD = N_Q_HEADS * HEAD_DIM
KD = N_KV_HEADS * HEAD_DIM
WINDOW = 128
ROPE_DIM = 16
ROPE_THETA = 500000.0
D_INNER = 2048
SSM_HEAD_DIM = 64
SSM_HEADS = 32
SSM_STATE = 128
SSM_GROUPS = 4
HEADS_PER_GROUP = SSM_HEADS // SSM_GROUPS
GROUP_WIDTH = D_INNER // SSM_GROUPS
GN = SSM_GROUPS * SSM_STATE
SSM_CONV_WIDTH = 4
SSM_CONV_DIM = D_INNER + 2 * GN
SSD_CHUNK = 128
D_FF = 2816
FFN_CONV_WIDTH = 3
EPS = 1e-6

V7X_VMEM_LIMIT_BYTES = 58 * 1024 * 1024
SUBLANES = 8
LANES = 128
Q_SLABS = QD // LANES
X_SLABS = D_INNER // LANES
G_SLABS = GN // LANES
SLABS_PER_GROUP = GROUP_WIDTH // LANES

ROW_TILE = 512
SAMPLE_SEQ_BLOCK = 8


def _params(n_grid_dims):
    return pltpu.CompilerParams(
        dimension_semantics=("arbitrary",) * n_grid_dims,
        vmem_limit_bytes=V7X_VMEM_LIMIT_BYTES,
    )


def _resident():
    return pl.BlockSpec(memory_space=pltpu.VMEM)


def _rms(x, g_row):
    ms = jnp.mean(x * x, axis=-1, keepdims=True)
    return (x * lax.rsqrt(ms + EPS)) * g_row


def _sigmoid(x):
    return 1.0 / (1.0 + jnp.exp(-x))


def _silu(x):
    return x * _sigmoid(x)


def _softplus(x):
    return jnp.maximum(x, 0.0) + jnp.log(1.0 + jnp.exp(-jnp.abs(x)))


def _dot(a, b):
    return jnp.dot(a, b, preferred_element_type=F32)


def _dot_f32(a, b):
    return jnp.dot(a, b, preferred_element_type=F32, precision=lax.Precision.HIGHEST)


def _dot_nt(a, b):
    return lax.dot_general(a, b, (((1,), (1,)), ((), ())), preferred_element_type=F32)


def _dot_tn(a, b):
    return lax.dot_general(a, b, (((0,), (0,)), ((), ())), preferred_element_type=F32)


def _dot_split(a, m_bf16):
    hi = a.astype(BF16)
    lo = (a - hi.astype(F32)).astype(BF16)
    return _dot(hi, m_bf16) + _dot(lo, m_bf16)


def _rope(x, ra, rb, rc):
    return x * ra + pltpu.roll(x, 8, 1) * rb + pltpu.roll(x, LANES - 8, 1) * rc


def _softmax_with_sink(s, sink):
    m = jnp.maximum(jnp.max(s, axis=-1, keepdims=True), sink)
    p = jnp.exp(s - m)
    den = jnp.sum(p, axis=-1, keepdims=True) + jnp.exp(sink - m)
    return p / den


def _slab_cat(ref, rows=slice(None)):
    return jnp.concatenate([ref[j, rows, :] for j in range(ref.shape[0])], axis=1)


def _ffn_kernel(x_ref, hist_ref, g_ref, wg_ref, wu_ref, cw_ref, cb_ref, wd_ref,
                out_ref, hout_ref, gbuf, act, *, tm, hb, stride, cw):
    @pl.when(pl.program_id(0) == 0)
    def _():
        gbuf[0:hb, :] = hist_ref[...]

    x = x_ref[...]
    h = _rms(x, g_ref[...]).astype(BF16)
    for j in range(D_FF // cw):
        cs = slice(j * cw, (j + 1) * cw)
        gate = _dot(h, wg_ref[:, cs])
        up = _dot(h, wu_ref[:, cs])
        gbuf[hb:hb + tm, cs] = gate
        conv = (cw_ref[0:1, cs] * gbuf[hb - 2 * stride:hb - 2 * stride + tm, cs]
                + cw_ref[1:2, cs] * gbuf[hb - stride:hb - stride + tm, cs]
                + cw_ref[2:3, cs] * gate + cb_ref[:, cs])
        act[:, cs] = (_silu(conv) * up).astype(BF16)
    out_ref[...] = x + _dot(act[...], wd_ref[...])
    tail = gbuf[tm:tm + hb, :]
    hout_ref[...] = tail
    gbuf[0:hb, :] = tail


def _ffn_call(x, hist, g, wg, wu, cwt, cbias, wd, *, stride):
    rows = x.shape[0]
    tm = ROW_TILE
    hb = (FFN_CONV_WIDTH - 1) * stride
    kern = functools.partial(_ffn_kernel, tm=tm, hb=hb, stride=stride, cw=256)
    return pl.pallas_call(
        kern,
        grid=(rows // tm,),
        in_specs=[
            pl.BlockSpec((tm, D_MODEL), lambda i: (i, 0)),
            pl.BlockSpec((hb, D_FF), lambda i: (0, 0)),
            _resident(), _resident(), _resident(), _resident(), _resident(), _resident(),
        ],
        out_specs=[
            pl.BlockSpec((tm, D_MODEL), lambda i: (i, 0)),
            pl.BlockSpec((hb, D_FF), lambda i: (0, 0)),
        ],
        out_shape=[
            jax.ShapeDtypeStruct((rows, D_MODEL), F32),
            jax.ShapeDtypeStruct((hb, D_FF), F32),
        ],
        scratch_shapes=[pltpu.VMEM((hb + tm, D_FF), F32), pltpu.VMEM((tm, D_FF), BF16)],
        compiler_params=_params(1),
        name="conv_ffn",
    )(x, hist, g, wg, wu, cwt, cbias, wd)


CONV_ROWS = 16


def _even_pre_kernel(x_ref, hist_ref, ra_ref, rb_ref, rc_ref, g_ref, win_ref, cw_ref, cb_ref,
                     lng_ref, lnb_ref, qg_ref, kg_ref, pmean_ref,
                     c_ref, q_ref, k_ref, v_ref, hout_ref, ubuf, cbuf, *, tm, hb, stride):
    @pl.when(pl.program_id(0) == 0)
    def _():
        ubuf[0:hb, :] = hist_ref[...]

    x = x_ref[...]
    h = _rms(x, g_ref[...]).astype(BF16)
    proj = _dot(h, win_ref[...])
    u = proj[:, 0:CONV_A_DIM] * _sigmoid(proj[:, CONV_A_DIM:2 * CONV_A_DIM])
    ubuf[hb:hb + tm, :] = u

    for r0 in range(0, tm, CONV_ROWS):
        acc = jnp.zeros((CONV_ROWS, CONV_A_DIM), F32)
        for j in range(CONV_A_WIDTH):
            off = r0 + j * stride
            acc = acc + cw_ref[j:j + 1, :] * ubuf[off:off + CONV_ROWS, :]
        cbuf[r0:r0 + CONV_ROWS, :] = acc + cb_ref[...]

    conv = cbuf[...]
    mu = jnp.mean(conv, axis=-1, keepdims=True)
    cen = conv - mu
    var = jnp.mean(cen * cen, axis=-1, keepdims=True)
    ln = cen * lax.rsqrt(var + EPS) * lng_ref[...] + lnb_ref[...]
    c_ref[...] = _silu(ln).astype(BF16)

    tail = ubuf[tm:tm + hb, :]
    hout_ref[...] = tail
    ubuf[0:hb, :] = tail

    ra, rb, rc = ra_ref[...], rb_ref[...], rc_ref[...]
    pmean = pmean_ref[...]
    q0 = 2 * CONV_A_DIM
    for s in range(Q_SLABS):
        qs = proj[:, q0 + s * LANES:q0 + (s + 1) * LANES]
        ms = _dot_split(qs * qs, pmean)
        qn = qs * lax.rsqrt(ms + EPS) * qg_ref[...]
        q_ref[s] = _rope(qn, ra, rb, rc)
    ks = proj[:, q0 + QD:q0 + QD + KD]
    ms = _dot_split(ks * ks, pmean)
    kn = ks * lax.rsqrt(ms + EPS) * kg_ref[...]
    k_ref[...] = _rope(kn, ra, rb, rc)
    v_ref[...] = proj[:, q0 + QD + KD:q0 + QD + 2 * KD]


def _even_pre_call(x, hist, ra, rb, rc, g, win, cwt, cbias, lng, lnb, qg, kg, pmean, *, stride):
    rows = x.shape[0]
    tm = ROW_TILE
    hb = (CONV_A_WIDTH - 1) * stride
    kern = functools.partial(_even_pre_kernel, tm=tm, hb=hb, stride=stride)
    row_spec = lambda w: pl.BlockSpec((tm, w), lambda i: (i, 0))
    return pl.pallas_call(
        kern,
        grid=(rows // tm,),
        in_specs=[
            row_spec(D_MODEL),
            pl.BlockSpec((hb, CONV_A_DIM), lambda i: (0, 0)),
            row_spec(LANES), row_spec(LANES), row_spec(LANES),
        ] + [_resident()] * 9,
        out_specs=[
            row_spec(CONV_A_DIM),
            pl.BlockSpec((Q_SLABS, tm, LANES), lambda i: (0, i, 0)),
            row_spec(KD), row_spec(KD),
            pl.BlockSpec((hb, CONV_A_DIM), lambda i: (0, 0)),
        ],
        out_shape=[
            jax.ShapeDtypeStruct((rows, CONV_A_DIM), BF16),
            jax.ShapeDtypeStruct((Q_SLABS, rows, LANES), F32),
            jax.ShapeDtypeStruct((rows, KD), F32),
            jax.ShapeDtypeStruct((rows, KD), F32),
            jax.ShapeDtypeStruct((hb, CONV_A_DIM), F32),
        ],
        scratch_shapes=[pltpu.VMEM((hb + tm, CONV_A_DIM), F32), pltpu.VMEM((tm, CONV_A_DIM), F32)],
        compiler_params=_params(1),
        name="even_pre",
    )(x, hist, ra, rb, rc, g, win, cwt, cbias, lng, lnb, qg, kg, pmean)


def _attn_out_kernel(sink_ref, q_ref, k_ref, v_ref, c_ref, x_ref, wout_ref,
                     out_ref, kprev, vprev, obuf, *, nseq):
    step = pl.program_id(0)

    @pl.when(step == 0)
    def _():
        kprev[...] = jnp.zeros_like(kprev)
        vprev[...] = jnp.zeros_like(vprev)

    row = lax.broadcasted_iota(jnp.int32, (WINDOW, 2 * WINDOW), 0)
    col = lax.broadcasted_iota(jnp.int32, (WINDOW, 2 * WINDOW), 1)
    valid = (col - row >= 1) & (col - row <= WINDOW) & ((col >= WINDOW) | (step > 0))

    def per_seq(b, carry):
        rows = pl.ds(b, WINDOW, stride=nseq)
        kb = k_ref[rows, :].astype(BF16)
        vb = v_ref[rows, :].astype(BF16)
        kk = jnp.concatenate([kprev[b], kb], axis=0)
        vv = jnp.concatenate([vprev[b], vb], axis=0)
        for j in range(Q_SLABS):
            qj = q_ref[j, rows, :].astype(BF16)
            outs = []
            for h2 in range(LANES // HEAD_DIM):
                hd = j * (LANES // HEAD_DIM) + h2
                ls = slice((hd // Q_PER_KV) * HEAD_DIM, (hd // Q_PER_KV + 1) * HEAD_DIM)
                s = _dot_nt(qj[:, h2 * HEAD_DIM:(h2 + 1) * HEAD_DIM], kk[:, ls]) * (HEAD_DIM ** -0.5)
                s = jnp.where(valid, s, -jnp.inf)
                p = _softmax_with_sink(s, sink_ref[hd]).astype(BF16)
                outs.append(_dot(p, vv[:, ls]))
            obuf[j, rows, :] = jnp.concatenate(outs, axis=1)
        kprev[b] = kb
        vprev[b] = vb
        return carry

    lax.fori_loop(0, nseq, per_seq, 0)
    o = _slab_cat(obuf).astype(BF16)
    mixed = _dot(c_ref[...], wout_ref[0:CONV_A_DIM, :]) + _dot(o, wout_ref[CONV_A_DIM:, :])
    out_ref[...] = x_ref[...] + mixed


def _attn_out_call(sinks, q, k, v, c, x, wout, *, nseq):
    rows = x.shape[0]
    tq = WINDOW * nseq
    kern = functools.partial(_attn_out_kernel, nseq=nseq)
    row_spec = lambda w: pl.BlockSpec((tq, w), lambda i: (i, 0))
    return pl.pallas_call(
        kern,
        grid=(rows // tq,),
        in_specs=[
            pl.BlockSpec(memory_space=pltpu.SMEM),
            pl.BlockSpec((Q_SLABS, tq, LANES), lambda i: (0, i, 0)),
            row_spec(KD), row_spec(KD), row_spec(CONV_A_DIM), row_spec(D_MODEL), _resident(),
        ],
        out_specs=row_spec(D_MODEL),
        out_shape=jax.ShapeDtypeStruct((rows, D_MODEL), F32),
        scratch_shapes=[pltpu.VMEM((nseq, WINDOW, KD), BF16), pltpu.VMEM((nseq, WINDOW, KD), BF16),
                        pltpu.VMEM((Q_SLABS, tq, LANES), F32)],
        compiler_params=_params(1),
        name="attn_out",
    )(sinks, q, k, v, c, x, wout)


def _sample_attn_kernel(sink_ref, q_ref, kn_ref, vn_ref, kh_ref, vh_ref, o_ref, *, t):
    rows = Q_PER_KV * t
    keys = WINDOW + t
    r = lax.broadcasted_iota(jnp.int32, (rows, keys), 0)
    j = lax.broadcasted_iota(jnp.int32, (rows, keys), 1)
    tq = r % t
    valid = (j > tq) & (j <= tq + WINDOW)
    grp = lax.broadcasted_iota(jnp.int32, (rows, 1), 0) // t
    for b in range(SAMPLE_SEQ_BLOCK):
        kall = jnp.concatenate([kh_ref[b], kn_ref[b]], axis=0).astype(BF16)
        vall = jnp.concatenate([vh_ref[b], vn_ref[b]], axis=0).astype(BF16)
        for kv in range(N_KV_HEADS):
            sink = jnp.zeros((rows, 1), F32)
            for g in range(Q_PER_KV):
                sink = jnp.where(grp == g, sink_ref[kv * Q_PER_KV + g], sink)
            ls = slice(kv * HEAD_DIM, (kv + 1) * HEAD_DIM)
            s = _dot_nt(q_ref[b, kv].astype(BF16), kall[:, ls]) * (HEAD_DIM ** -0.5)
            s = jnp.where(valid, s, -jnp.inf)
            p = _softmax_with_sink(s, sink).astype(BF16)
            o_ref[b, kv] = _dot(p, vall[:, ls])


def _sample_attn_call(sinks, q, kn, vn, kh, vh, layer):
    nb, _, rows, _ = q.shape
    t = kn.shape[1]
    bb = SAMPLE_SEQ_BLOCK
    kern = functools.partial(_sample_attn_kernel, t=t)
    hist_spec = pl.BlockSpec((None, bb, WINDOW, KD), lambda i: (layer, i, 0, 0))
    return pl.pallas_call(
        kern,
        grid=(nb // bb,),
        in_specs=[
            pl.BlockSpec(memory_space=pltpu.SMEM),
            pl.BlockSpec((bb, N_KV_HEADS, rows, HEAD_DIM), lambda i: (i, 0, 0, 0)),
            pl.BlockSpec((bb, t, KD), lambda i: (i, 0, 0)),
            pl.BlockSpec((bb, t, KD), lambda i: (i, 0, 0)),
            hist_spec, hist_spec,
        ],
        out_specs=pl.BlockSpec((bb, N_KV_HEADS, rows, HEAD_DIM), lambda i: (i, 0, 0, 0)),
        out_shape=jax.ShapeDtypeStruct((nb, N_KV_HEADS, rows, HEAD_DIM), F32),
        compiler_params=_params(1),
        name="sample_attn",
    )(sinks, q, kn, vn, kh, vh)


def _even_out_kernel(c_ref, o_ref, x_ref, wout_ref, out_ref):
    mixed = _dot(c_ref[...], wout_ref[0:CONV_A_DIM, :]) + _dot(o_ref[...].astype(BF16), wout_ref[CONV_A_DIM:, :])
    out_ref[...] = x_ref[...] + mixed


def _even_out_call(c, o, x, wout):
    rows = x.shape[0]
    tm = ROW_TILE
    row_spec = lambda w: pl.BlockSpec((tm, w), lambda i: (i, 0))
    return pl.pallas_call(
        _even_out_kernel,
        grid=(rows // tm,),
        in_specs=[row_spec(CONV_A_DIM), row_spec(QD), row_spec(D_MODEL), _resident()],
        out_specs=row_spec(D_MODEL),
        out_shape=jax.ShapeDtypeStruct((rows, D_MODEL), F32),
        compiler_params=_params(1),
        name="even_out",
    )(c, o, x, wout)


def _odd_pre_kernel(x_ref, hist_ref, g_ref, wz_ref, wx_ref, wdt_ref, cw_ref, cb_ref, dtb_ref,
                    z_ref, xs_ref, b_ref, c_ref, dt_ref, hout_ref, xbuf, *, tm, hb, stride, cw):
    @pl.when(pl.program_id(0) == 0)
    def _():
        xbuf[0:hb, :] = hist_ref[...]

    h = _rms(x_ref[...], g_ref[...]).astype(BF16)
    for j in range(D_INNER // cw):
        cs = slice(j * cw, (j + 1) * cw)
        z_ref[:, cs] = _dot(h, wz_ref[:, cs])
    spb = cw // LANES
    for j in range(SSM_CONV_DIM // cw):
        cs = slice(j * cw, (j + 1) * cw)
        pre = _dot(h, wx_ref[:, cs])
        xbuf[hb:hb + tm, cs] = pre
        conv = cw_ref[SSM_CONV_WIDTH - 1:SSM_CONV_WIDTH, cs] * pre + cb_ref[:, cs]
        for tap in range(SSM_CONV_WIDTH - 1):
            off = hb - (SSM_CONV_WIDTH - 1 - tap) * stride
            conv = conv + cw_ref[tap:tap + 1, cs] * xbuf[off:off + tm, cs]
        act = _silu(conv)
        for q in range(spb):
            slab = j * spb + q
            piece = act[:, q * LANES:(q + 1) * LANES]
            if slab < X_SLABS:
                xs_ref[slab] = piece
            elif slab < X_SLABS + G_SLABS:
                b_ref[slab - X_SLABS] = piece
            else:
                c_ref[slab - X_SLABS - G_SLABS] = piece
    dt_ref[...] = _softplus(_dot(h, wdt_ref[...]) + dtb_ref[...])
    tail = xbuf[tm:tm + hb, :]
    hout_ref[...] = tail
    xbuf[0:hb, :] = tail


def _odd_pre_call(x, hist, g, wz, wx, wdt, cwt, cbias, dtb, *, stride):
    rows = x.shape[0]
    tm = ROW_TILE
    hb = (SSM_CONV_WIDTH - 1) * stride
    kern = functools.partial(_odd_pre_kernel, tm=tm, hb=hb, stride=stride, cw=512)
    row_spec = lambda w: pl.BlockSpec((tm, w), lambda i: (i, 0))
    slab_spec = lambda n: pl.BlockSpec((n, tm, LANES), lambda i: (0, i, 0))
    return pl.pallas_call(
        kern,
        grid=(rows // tm,),
        in_specs=[
            row_spec(D_MODEL),
            pl.BlockSpec((hb, SSM_CONV_DIM), lambda i: (0, 0)),
        ] + [_resident()] * 7,
        out_specs=[
            row_spec(D_INNER), slab_spec(X_SLABS), slab_spec(G_SLABS), slab_spec(G_SLABS), row_spec(SSM_HEADS),
            pl.BlockSpec((hb, SSM_CONV_DIM), lambda i: (0, 0)),
        ],
        out_shape=[
            jax.ShapeDtypeStruct((rows, D_INNER), F32),
            jax.ShapeDtypeStruct((X_SLABS, rows, LANES), F32),
            jax.ShapeDtypeStruct((G_SLABS, rows, LANES), F32),
            jax.ShapeDtypeStruct((G_SLABS, rows, LANES), F32),
            jax.ShapeDtypeStruct((rows, SSM_HEADS), F32),
            jax.ShapeDtypeStruct((hb, SSM_CONV_DIM), F32),
        ],
        scratch_shapes=[pltpu.VMEM((hb + tm, SSM_CONV_DIM), F32)],
        compiler_params=_params(1),
        name="odd_pre",
    )(x, hist, g, wz, wx, wdt, cwt, cbias, dtb)


def _ssd_kernel(*refs, nseq, has_prev):
    (xs_ref, b_ref, c_ref, dt_ref, alog_ref, alogt_ref, tril_ref, triu_ref, expand_ref) = refs[:9]
    y_ref, ht = refs[9 + has_prev], refs[10 + has_prev]
    dtbuf = refs[11 + has_prev]
    L = SSD_CHUNK
    N = SSM_STATE
    step = pl.program_id(0)

    @pl.when(step == 0)
    def _():
        ht[...] = jnp.zeros_like(ht)
        dtbuf[...] = jnp.zeros_like(dtbuf)

    dtbuf[:, 0:SSM_HEADS] = dt_ref[...]
    a_row = -jnp.exp(alog_ref[...])
    a_col = -jnp.exp(alogt_ref[...])
    tril, triu, expand = tril_ref[...], triu_ref[...], expand_ref[...]
    row = lax.broadcasted_iota(jnp.int32, (L, L), 0)
    col = lax.broadcasted_iota(jnp.int32, (L, L), 1)
    causal = row >= col
    lane = lax.broadcasted_iota(jnp.int32, (L, LANES), 1)
    low_half = lane < SSM_HEAD_DIM

    def per_seq(b, carry):
        rows = pl.ds(b, L, stride=nseq)
        dt_wide = dtbuf[rows, :]
        dt = dt_wide[:, 0:SSM_HEADS]
        dtt = dt_wide.T[0:SSM_HEADS, :]
        acum = _dot_f32(tril, dt * a_row)
        acumt = _dot_f32(dtt * a_col, triu)
        last = acum[L - 1:L, :]
        ea = _dot_split(jnp.exp(acum), expand)
        de = _dot_split(jnp.exp(last - acum) * dt, expand)
        cd = _dot_split(jnp.broadcast_to(jnp.exp(last), (SUBLANES, SSM_HEADS)), expand)[0:1, :]

        cbs, y_off, b_t = [], [], []
        for g in range(SSM_GROUPS):
            cg = c_ref[g, rows, :].astype(BF16)
            bg = b_ref[g, rows, :]
            cbs.append(_dot_nt(cg, bg.astype(BF16)))
            y_off.append(_dot(cg, ht[b, g * N:(g + 1) * N, :].astype(BF16)))
            b_t.append(bg.T.astype(BF16))

        xw = []
        for slab in range(X_SLABS):
            g = slab // SLABS_PER_GROUP
            ws = []
            for hd in (2 * slab, 2 * slab + 1):
                diff = jnp.where(causal, acum[:, hd:hd + 1] - acumt[hd:hd + 1, :], -jnp.inf)
                ws.append(cbs[g] * jnp.exp(diff) * dtt[hd:hd + 1, :])
            wcat = jnp.concatenate(ws, axis=1).astype(BF16)
            xpair = xs_ref[slab, rows, :]
            xbd = jnp.concatenate([jnp.where(low_half, xpair, 0.0), jnp.where(low_half, 0.0, xpair)], axis=0)
            ls = slice(slab * LANES, (slab + 1) * LANES)
            gl = slice((slab % SLABS_PER_GROUP) * LANES, (slab % SLABS_PER_GROUP + 1) * LANES)
            y_ref[slab, rows, :] = _dot(wcat, xbd.astype(BF16)) + ea[:, ls] * y_off[g][:, gl]
            xw.append((xpair * de[:, ls]).astype(BF16))

        for g in range(SSM_GROUPS):
            xw_g = jnp.concatenate(xw[g * SLABS_PER_GROUP:(g + 1) * SLABS_PER_GROUP], axis=1)
            gs = slice(g * N, (g + 1) * N)
            ht[b, gs, :] = cd[:, g * GROUP_WIDTH:(g + 1) * GROUP_WIDTH] * ht[b, gs, :] + _dot(b_t[g], xw_g)
        return carry

    lax.fori_loop(0, nseq, per_seq, 0)


def _ssd_call(xs, bm, cm, dt, alog, alogt, tril, triu, expand, prev_state, *, nseq, layer, n_layers):
    rows = dt.shape[0]
    tq = SSD_CHUNK * nseq
    has_prev = prev_state is not None
    kern = functools.partial(_ssd_kernel, nseq=nseq, has_prev=int(has_prev))
    slab_spec = lambda n: pl.BlockSpec((n, tq, LANES), lambda i: (0, i, 0))
    in_specs = [
        pl.BlockSpec((X_SLABS, tq, LANES), lambda i: (0, i, 0), pipeline_mode=pl.Buffered(1)),
        slab_spec(G_SLABS), slab_spec(G_SLABS),
        pl.BlockSpec((tq, SSM_HEADS), lambda i: (i, 0)),
    ] + [_resident()] * 5
    args = [xs, bm, cm, dt, alog, alogt, tril, triu, expand]
    aliases = {}
    if has_prev:
        in_specs.append(pl.BlockSpec(memory_space=pl.ANY))
        args.append(prev_state)
        aliases = {len(args) - 1: 1}
    return pl.pallas_call(
        kern,
        grid=(rows // tq,),
        in_specs=in_specs,
        out_specs=[
            slab_spec(X_SLABS),
            pl.BlockSpec((None, nseq, GN, GROUP_WIDTH), lambda i: (layer, 0, 0, 0)),
        ],
        out_shape=[
            jax.ShapeDtypeStruct((X_SLABS, rows, LANES), F32),
            jax.ShapeDtypeStruct((n_layers, nseq, GN, GROUP_WIDTH), F32),
        ],
        scratch_shapes=[pltpu.VMEM((tq, LANES), F32)],
        input_output_aliases=aliases,
        compiler_params=_params(1),
        name="ssd_chunk",
    )(*args)


def _sample_ssd_intra_kernel(xs_ref, b_ref, c_ref, dt_ref, alog_ref, expand_ref, gexp_ref,
                             ydiag_ref, xw_ref, ea_ref, cd_ref, *, t, nb):
    a_row = -jnp.exp(alog_ref[...])
    expand = expand_ref[...]
    gexp = gexp_ref[...]
    blk = lambda l: slice(l * nb, (l + 1) * nb)
    dts = [dt_ref[blk(l), :] for l in range(t)]
    acum = []
    run = None
    for l in range(t):
        step = dts[l] * a_row
        run = step if run is None else run + step
        acum.append(run)
    last = acum[t - 1]
    cd_ref[...] = _dot_split(jnp.exp(last), expand)
    for l in range(t):
        ea_ref[blk(l), :] = _dot_split(jnp.exp(acum[l]), expand)
        xw_ref[blk(l), :] = _slab_cat(xs_ref, blk(l)) * _dot_split(jnp.exp(last - acum[l]) * dts[l], expand)
        yd = jnp.zeros((nb, D_INNER), F32)
        c_l = _slab_cat(c_ref, blk(l))
        for s in range(l + 1):
            cbx = _dot((c_l * _slab_cat(b_ref, blk(s))).astype(BF16), gexp)
            w = _dot_split(jnp.exp(acum[l] - acum[s]) * dts[s], expand)
            yd = yd + cbx * w * _slab_cat(xs_ref, blk(s))
        ydiag_ref[blk(l), :] = yd


def _sample_ssd_intra_call(xs, bm, cm, dt, alog, expand, gexp, *, t, nb):
    rows = dt.shape[0]
    kern = functools.partial(_sample_ssd_intra_kernel, t=t, nb=nb)
    return pl.pallas_call(
        kern,
        in_specs=[_resident()] * 7,
        out_specs=[_resident()] * 4,
        out_shape=[
            jax.ShapeDtypeStruct((rows, D_INNER), F32),
            jax.ShapeDtypeStruct((rows, D_INNER), F32),
            jax.ShapeDtypeStruct((rows, D_INNER), F32),
            jax.ShapeDtypeStruct((nb, D_INNER), F32),
        ],
        compiler_params=pltpu.CompilerParams(vmem_limit_bytes=V7X_VMEM_LIMIT_BYTES),
        name="sample_ssd_intra",
    )(xs, bm, cm, dt, alog, expand, gexp)


def _sample_ssd_state_kernel(*refs, has_prev):
    h0_ref, c_ref, b_ref, xw_ref, ydiag_ref, ea_ref, cd_ref = refs[:7]
    y_ref, hnew_ref = refs[7 + has_prev], refs[8 + has_prev]
    for j in range(SAMPLE_SEQ_BLOCK):
        ht = h0_ref[j].T
        xw = xw_ref[:, j, :].astype(BF16)
        y_parts, new_parts = [], []
        for g in range(SSM_GROUPS):
            ws_ = slice(g * GROUP_WIDTH, (g + 1) * GROUP_WIDTH)
            y_parts.append(_dot(c_ref[g, :, j, :].astype(BF16), ht[:, ws_].astype(BF16)))
            new_parts.append(_dot_tn(b_ref[g, :, j, :].astype(BF16), xw[:, ws_]))
        y_ref[:, j, :] = ydiag_ref[:, j, :] + ea_ref[:, j, :] * jnp.concatenate(y_parts, axis=1)
        hnew = cd_ref[j:j + 1, :] * ht + jnp.concatenate(new_parts, axis=1)
        hnew_ref[j] = hnew.T


def _sample_ssd_state_call(h0_all, cm, bm, xw, ydiag, ea, cd, prev_state, *, layer):
    n_layers, nb = h0_all.shape[:2]
    t = xw.shape[0]
    bb = SAMPLE_SEQ_BLOCK
    has_prev = prev_state is not None
    kern = functools.partial(_sample_ssd_state_kernel, has_prev=int(has_prev))
    tspec = lambda w: pl.BlockSpec((t, bb, w), lambda i: (0, i, 0))
    gspec = pl.BlockSpec((G_SLABS, t, bb, LANES), lambda i: (0, 0, i, 0))
    state_spec = pl.BlockSpec((None, bb, D_INNER, SSM_STATE), lambda i: (layer, i, 0, 0))
    in_specs = [state_spec, gspec, gspec, tspec(D_INNER), tspec(D_INNER), tspec(D_INNER),
                pl.BlockSpec((bb, D_INNER), lambda i: (i, 0))]
    args = [h0_all, cm, bm, xw, ydiag, ea, cd]
    aliases = {}
    if has_prev:
        in_specs.append(pl.BlockSpec(memory_space=pl.ANY))
        args.append(prev_state)
        aliases = {len(args) - 1: 1}
    return pl.pallas_call(
        kern,
        grid=(nb // bb,),
        in_specs=in_specs,
        out_specs=[tspec(D_INNER), state_spec],
        out_shape=[
            jax.ShapeDtypeStruct((t, nb, D_INNER), F32),
            jax.ShapeDtypeStruct((n_layers, nb, D_INNER, SSM_STATE), F32),
        ],
        input_output_aliases=aliases,
        compiler_params=_params(1),
        name="sample_ssd_state",
    )(*args)


def _odd_out_kernel(y_ref, xs_ref, z_ref, x_ref, dskip_ref, gn_ref, wout_ref, out_ref, *, y_slabs):
    y = _slab_cat(y_ref) if y_slabs else y_ref[...]
    v = (y + dskip_ref[...] * _slab_cat(xs_ref)) * _silu(z_ref[...])
    parts = []
    for g in range(SSM_GROUPS):
        vg = v[:, g * GROUP_WIDTH:(g + 1) * GROUP_WIDTH]
        ms = jnp.mean(vg * vg, axis=-1, keepdims=True)
        parts.append(vg * lax.rsqrt(ms + EPS))
    vn = (jnp.concatenate(parts, axis=1) * gn_ref[...]).astype(BF16)
    out_ref[...] = x_ref[...] + _dot(vn, wout_ref[...])


def _odd_out_call(y, xs, z, x, dskip, gn, wout):
    rows = x.shape[0]
    tm = ROW_TILE
    y_slabs = y.ndim == 3
    row_spec = lambda w: pl.BlockSpec((tm, w), lambda i: (i, 0))
    slab_spec = pl.BlockSpec((X_SLABS, tm, LANES), lambda i: (0, i, 0))
    return pl.pallas_call(
        functools.partial(_odd_out_kernel, y_slabs=y_slabs),
        grid=(rows // tm,),
        in_specs=[slab_spec if y_slabs else row_spec(D_INNER), slab_spec, row_spec(D_INNER), row_spec(D_MODEL),
                  _resident(), _resident(), _resident()],
        out_specs=row_spec(D_MODEL),
        out_shape=jax.ShapeDtypeStruct((rows, D_MODEL), F32),
        compiler_params=_params(1),
        name="odd_out",
    )(y, xs, z, x, dskip, gn, wout)


def _rope_tables(pos):
    half = ROPE_DIM // 2
    inv_freq = ROPE_THETA ** (-jnp.arange(0, ROPE_DIM, 2, dtype=F32) / ROPE_DIM)
    ang = pos.astype(F32)[:, None] * inv_freq[None, :]
    cos, sin = jnp.cos(ang), jnp.sin(ang)
    n = pos.shape[0]
    rest = HEAD_DIM - ROPE_DIM
    ra = jnp.concatenate([cos, cos, jnp.ones((n, rest), F32)], axis=1)
    rb = jnp.concatenate([jnp.zeros((n, half), F32), sin, jnp.zeros((n, rest), F32)], axis=1)
    rc = jnp.concatenate([-sin, jnp.zeros((n, half + rest), F32)], axis=1)
    rep = LANES // HEAD_DIM
    return tuple(jnp.tile(m, (1, rep)) for m in (ra, rb, rc))


def _head_mean_matrix():
    idx = jnp.arange(LANES) // HEAD_DIM
    return ((idx[:, None] == idx[None, :]).astype(F32) / HEAD_DIM).astype(BF16)


def _head_expand_matrix():
    idx = jnp.arange(D_INNER) // SSM_HEAD_DIM
    return (jnp.arange(SSM_HEADS)[:, None] == idx[None, :]).astype(BF16)


def _group_expand_matrix():
    src = jnp.arange(GN) // SSM_STATE
    dst = jnp.arange(D_INNER) // GROUP_WIDTH
    return (src[:, None] == dst[None, :]).astype(BF16)


def _make_tables(tp, bp, ts, nb):
    tril = jnp.tril(jnp.ones((SSD_CHUNK, SSD_CHUNK), F32))
    return {
        "rope_p": _rope_tables(jnp.repeat(jnp.arange(tp, dtype=jnp.int32), bp)),
        "rope_s": _rope_tables(PAST_LEN + jnp.repeat(jnp.arange(ts, dtype=jnp.int32), nb)),
        "pmean": _head_mean_matrix(),
        "expand": _head_expand_matrix(),
        "gexp": _group_expand_matrix(),
        "tril": tril,
        "triu": tril.T,
    }


def _row(v):
    return v.reshape(1, -1)


def _time_major_hist(state):
    return jnp.swapaxes(state, 0, 1).reshape(-1, state.shape[-1])


def _batch_major_hist(rows, nseq):
    return jnp.swapaxes(rows.reshape(-1, nseq, rows.shape[-1]), 0, 1)


def _even_layer(xp, xs, conv_hist_s, k_hist_all, v_hist_all, layer, norm_g, w_in, conv_w, conv_b, ln_g, ln_b,
                qn_g, kn_g, sinks, w_out, tables, *, bp, tp, nb, ts):
    win = w_in.astype(BF16)
    wout = w_out.astype(BF16)
    shared = (_row(norm_g), win, conv_w, _row(conv_b), _row(ln_g), _row(ln_b),
              _row(jnp.tile(qn_g, LANES // HEAD_DIM)), _row(jnp.tile(kn_g, LANES // HEAD_DIM)), tables["pmean"])
    keep = CONV_A_WIDTH - 1

    c, q, k, v, hout = _even_pre_call(xp, jnp.zeros((keep * bp, CONV_A_DIM), F32), *tables["rope_p"], *shared,
                                      stride=bp)
    xp_new = _attn_out_call(sinks, q, k, v, c, xp, wout, nseq=bp)
    conv_p = _batch_major_hist(hout, bp)
    last = lambda a: _batch_major_hist(a[-WINDOW * bp:], bp).reshape(bp, WINDOW, N_KV_HEADS, HEAD_DIM)
    wk_p, wv_p = last(k), last(v)

    c, q, k, v, hout = _even_pre_call(xs, _time_major_hist(conv_hist_s), *tables["rope_s"], *shared, stride=nb)
    conv_s = _batch_major_hist(hout, nb)
    k_new = _batch_major_hist(k, nb)
    v_new = _batch_major_hist(v, nb)
    per_slab = LANES // HEAD_DIM
    q_b = q.reshape(N_KV_HEADS, Q_SLABS // N_KV_HEADS, ts, nb, per_slab, HEAD_DIM).transpose(3, 0, 1, 4, 2, 5)
    q_b = q_b.reshape(nb, N_KV_HEADS, Q_PER_KV * ts, HEAD_DIM)
    kh_all = k_hist_all.reshape(-1, nb, WINDOW, KD)
    vh_all = v_hist_all.reshape(-1, nb, WINDOW, KD)
    o = _sample_attn_call(sinks, q_b, k_new, v_new, kh_all, vh_all, layer)
    o = o.reshape(nb, N_KV_HEADS, Q_PER_KV, ts, HEAD_DIM).transpose(3, 0, 1, 2, 4).reshape(ts * nb, QD)
    xs_new = _even_out_call(c, o, xs, wout)
    win_s = lambda hist, new: jnp.concatenate([hist[layer], new], axis=1)[:, -WINDOW:].reshape(
        nb, WINDOW, N_KV_HEADS, HEAD_DIM)
    return xp_new, xs_new, conv_p, conv_s, wk_p, win_s(kh_all, k_new), wv_p, win_s(vh_all, v_new)


def _odd_layer(xp, xs, conv_hist_s, ssm_all, layer, state_p, state_s, norm_g, w_in, conv_w, conv_b, dt_bias, a_log,
               d_skip, gn_g, w_out, tables, *, bp, tp, nb, ts):
    wz = w_in[:, :D_INNER].astype(BF16)
    wx = w_in[:, D_INNER:D_INNER + SSM_CONV_DIM].astype(BF16)
    wdt = w_in[:, D_INNER + SSM_CONV_DIM:].astype(BF16)
    wout = w_out.astype(BF16)
    dskip = _row(jnp.repeat(d_skip, SSM_HEAD_DIM))
    gn = _row(gn_g)
    alog = _row(a_log)
    pre_shared = (_row(norm_g), wz, wx, wdt, conv_w, _row(conv_b), _row(dt_bias))
    keep = SSM_CONV_WIDTH - 1
    n_layers = ssm_all.shape[0]

    z, xc, bm, cm, dt, hout = _odd_pre_call(xp, jnp.zeros((keep * bp, SSM_CONV_DIM), F32), *pre_shared, stride=bp)
    y, state_p = _ssd_call(xc, bm, cm, dt, alog, a_log.reshape(-1, 1), tables["tril"], tables["triu"],
                           tables["expand"], state_p, nseq=bp, layer=layer, n_layers=n_layers)
    xp_new = _odd_out_call(y, xc, z, xp, dskip, gn, wout)
    conv_p = _batch_major_hist(hout, bp)

    z, xc, bm, cm, dt, hout = _odd_pre_call(xs, _time_major_hist(conv_hist_s), *pre_shared, stride=nb)
    conv_s = _batch_major_hist(hout, nb)
    ydiag, xw, ea, cd = _sample_ssd_intra_call(xc, bm, cm, dt, alog, tables["expand"], tables["gexp"], t=ts, nb=nb)
    tm3 = lambda a: a.reshape(ts, nb, a.shape[-1])
    slab4 = lambda a: a.reshape(G_SLABS, ts, nb, LANES)
    y, state_s = _sample_ssd_state_call(ssm_all, slab4(cm), slab4(bm), tm3(xw), tm3(ydiag), tm3(ea), cd, state_s,
                                        layer=layer)
    xs_new = _odd_out_call(y.reshape(ts * nb, D_INNER), xc, z, xs, dskip, gn, wout)
    return xp_new, xs_new, conv_p, conv_s, state_p, state_s


def _ffn_layer(xp, xs, hist_s, norm_g, w_gate, w_up, conv_w, conv_b, w_down, *, bp, nb):
    keep = FFN_CONV_WIDTH - 1
    shared = (_row(norm_g), w_gate.astype(BF16), w_up.astype(BF16), conv_w, _row(conv_b), w_down.astype(BF16))
    xp_new, hout = _ffn_call(xp, jnp.zeros((keep * bp, D_FF), F32), *shared, stride=bp)
    ffn_p = _batch_major_hist(hout, bp)
    xs_new, hout = _ffn_call(xs, _time_major_hist(hist_s), *shared, stride=nb)
    ffn_s = _batch_major_hist(hout, nb)
    return xp_new, xs_new, ffn_p, ffn_s


def kernel(x_prompt, x_sample, state_conv_a, cache_win_k, cache_win_v, state_conv_c, state_ssm, state_ffn_conv,
           norm_mix_e, w_in_e, conv_a_w, conv_a_b, ln_a_g, ln_a_b, q_norm_g, k_norm_g, sinks, w_out_e,
           norm_mix_o, w_in_o, conv_c_w, conv_c_b, dt_bias, a_log, d_skip, gnorm_c, w_out_o,
           norm_ffn, w_gate, w_up, ffn_conv_w, ffn_conv_b, w_down):
    bp, tp, _ = x_prompt.shape
    nb, ts, _ = x_sample.shape
    depth = norm_ffn.shape[0]
    n_odd = state_ssm.shape[0]
    tables = _make_tables(tp, bp, ts, nb)
    dims = dict(bp=bp, tp=tp, nb=nb, ts=ts)

    xp = jnp.swapaxes(x_prompt, 0, 1).reshape(tp * bp, D_MODEL)
    xs = jnp.swapaxes(x_sample, 0, 1).reshape(ts * nb, D_MODEL)
    ssm_all = state_ssm.reshape(n_odd, nb, D_INNER, SSM_STATE)
    state_p = state_s = None
    ca_p, ca_s, wk_p, wk_s, wv_p, wv_s = [], [], [], [], [], []
    cc_p, cc_s, ff_p, ff_s = [], [], [], []
    for layer in range(depth):
        i = layer // 2
        if layer % 2 == 0:
            xp, xs, c1, c2, k1, k2, v1, v2 = _even_layer(
                xp, xs, state_conv_a[i], cache_win_k, cache_win_v, i, norm_mix_e[i], w_in_e[i], conv_a_w[i],
                conv_a_b[i], ln_a_g[i], ln_a_b[i], q_norm_g[i], k_norm_g[i], sinks[i], w_out_e[i], tables, **dims)
            ca_p.append(c1); ca_s.append(c2); wk_p.append(k1); wk_s.append(k2); wv_p.append(v1); wv_s.append(v2)
        else:
            xp, xs, c1, c2, state_p, state_s = _odd_layer(
                xp, xs, state_conv_c[i], ssm_all, i, state_p, state_s, norm_mix_o[i], w_in_o[i], conv_c_w[i],
                conv_c_b[i], dt_bias[i], a_log[i], d_skip[i], gnorm_c[i], w_out_o[i], tables, **dims)
            cc_p.append(c1); cc_s.append(c2)
        xp, xs, f1, f2 = _ffn_layer(xp, xs, state_ffn_conv[layer], norm_ffn[layer], w_gate[layer], w_up[layer],
                                    ffn_conv_w[layer], ffn_conv_b[layer], w_down[layer], bp=bp, nb=nb)
        ff_p.append(f1); ff_s.append(f2)

    y_prompt = jnp.swapaxes(xp.reshape(tp, bp, D_MODEL), 0, 1)
    y_sample = jnp.swapaxes(xs.reshape(ts, nb, D_MODEL), 0, 1)
    ssm_p = state_p.reshape(n_odd, bp, SSM_GROUPS, SSM_STATE, HEADS_PER_GROUP, SSM_HEAD_DIM)
    ssm_p = ssm_p.transpose(0, 1, 2, 4, 5, 3).reshape(n_odd, bp, SSM_HEADS, SSM_HEAD_DIM, SSM_STATE)
    ssm_s = state_s.reshape(n_odd, nb, SSM_HEADS, SSM_HEAD_DIM, SSM_STATE)
    return (y_prompt, y_sample,
            jnp.stack(ca_p), jnp.stack(ca_s),
            jnp.stack(wk_p), jnp.stack(wk_s),
            jnp.stack(wv_p), jnp.stack(wv_s),
            jnp.stack(cc_p), jnp.stack(cc_s),
            ssm_p, ssm_s,
            jnp.stack(ff_p), jnp.stack(ff_s))
```

```python
import functools

import jax
import jax.numpy as jnp
from jax import lax
from jax.experimental import pallas as pl
from jax.experimental.pallas import tpu as pltpu

F32 = jnp.float32
BF16 = jnp.bfloat16

D_MODEL = 1024
PAST_LEN = 8192
CONV_A_DIM = 512
CONV_A_WIDTH = 31
HEAD_DIM = 64
N_Q_HEADS = 8
N_KV_HEADS = 2
Q_PER_KV = N_Q_HEADS // N_KV_HEADS
QD = N_Q_HEADS * HEAD_DIM
KD = N_KV_HEADS * HEAD_DIM
WINDOW = 128
ROPE_DIM = 16
ROPE_THETA = 500000.0
D_INNER = 2048
SSM_HEAD_DIM = 64
SSM_HEADS = 32
SSM_STATE = 128
SSM_GROUPS = 4
HEADS_PER_GROUP = SSM_HEADS // SSM_GROUPS
GROUP_WIDTH = D_INNER // SSM_GROUPS
GN = SSM_GROUPS * SSM_STATE
SSM_CONV_WIDTH = 4
SSM_CONV_DIM = D_INNER + 2 * GN
SSD_CHUNK = 128
D_FF = 2816
FFN_CONV_WIDTH = 3
EPS = 1e-6

V7X_VMEM_LIMIT_BYTES = 58 * 1024 * 1024
SUBLANES = 8
LANES = 128
Q_SLABS = QD // LANES
X_SLABS = D_INNER // LANES
G_SLABS = GN // LANES
SLABS_PER_GROUP = GROUP_WIDTH // LANES

ROW_TILE = 512
SAMPLE_SEQ_BLOCK = 8


def _params(n_grid_dims):
    return pltpu.CompilerParams(
        dimension_semantics=("arbitrary",) * n_grid_dims,
        vmem_limit_bytes=V7X_VMEM_LIMIT_BYTES,
    )


def _resident():
    return pl.BlockSpec(memory_space=pltpu.VMEM)


def _layer_weight(stacked, layer):
    _, k, n = stacked.shape
    return pl.BlockSpec((None, k, n), lambda i: (layer, 0, 0), pipeline_mode=pl.Buffered(1))


def _w(ref, rows=slice(None), cols=slice(None)):
    return ref[rows, cols].astype(BF16)


def _row_tile(stride):
    return ROW_TILE if stride <= SUBLANES else ROW_TILE // 2


def _hist_spec(hb, width):
    return pl.BlockSpec((hb, width), lambda i: (0, 0), pipeline_mode=pl.Buffered(1))


def _rms(x, g_row):
    ms = jnp.mean(x * x, axis=-1, keepdims=True)
    return (x * lax.rsqrt(ms + EPS)) * g_row


def _sigmoid(x):
    return 1.0 / (1.0 + jnp.exp(-x))


def _silu(x):
    return x * _sigmoid(x)


def _softplus(x):
    return jnp.maximum(x, 0.0) + jnp.log(1.0 + jnp.exp(-jnp.abs(x)))


def _dot(a, b):
    return jnp.dot(a, b, preferred_element_type=F32)


def _dot_f32(a, b):
    return jnp.dot(a, b, preferred_element_type=F32, precision=lax.Precision.HIGHEST)


def _dot_nt(a, b):
    return lax.dot_general(a, b, (((1,), (1,)), ((), ())), preferred_element_type=F32)


def _dot_tn(a, b):
    return lax.dot_general(a, b, (((0,), (0,)), ((), ())), preferred_element_type=F32)


def _dot_split(a, m_bf16):
    hi = a.astype(BF16)
    lo = (a - hi.astype(F32)).astype(BF16)
    return _dot(hi, m_bf16) + _dot(lo, m_bf16)


def _rope(x, ra, rb, rc):
    return x * ra + pltpu.roll(x, 8, 1) * rb + pltpu.roll(x, LANES - 8, 1) * rc


def _softmax_with_sink(s, sink):
    m = jnp.maximum(jnp.max(s, axis=-1, keepdims=True), sink)
    p = jnp.exp(s - m)
    den = jnp.sum(p, axis=-1, keepdims=True) + jnp.exp(sink - m)
    return p / den


def _slab_cat(ref, rows=slice(None)):
    return jnp.concatenate([ref[j, rows, :] for j in range(ref.shape[0])], axis=1)


def _ffn_kernel(x_ref, hist_ref, g_ref, wg_ref, wu_ref, cw_ref, cb_ref, wd_ref,
                out_ref, hout_ref, gbuf, act, *, tm, hb, stride, cw):
    @pl.when(pl.program_id(0) == 0)
    def _():
        gbuf[0:hb, :] = hist_ref[...]

    x = x_ref[...]
    h = _rms(x, g_ref[...]).astype(BF16)
    for j in range(D_FF // cw):
        cs = slice(j * cw, (j + 1) * cw)
        gate = _dot(h, _w(wg_ref, cols=cs))
        up = _dot(h, _w(wu_ref, cols=cs))
        gbuf[hb:hb + tm, cs] = gate
        conv = (cw_ref[0:1, cs] * gbuf[hb - 2 * stride:hb - 2 * stride + tm, cs]
                + cw_ref[1:2, cs] * gbuf[hb - stride:hb - stride + tm, cs]
                + cw_ref[2:3, cs] * gate + cb_ref[:, cs])
        act[:, cs] = (_silu(conv) * up).astype(BF16)
    y = x
    kb = D_FF // 2
    for j in range(2):
        ks = slice(j * kb, (j + 1) * kb)
        y = y + _dot(act[:, ks], _w(wd_ref, rows=ks))
    out_ref[...] = y
    tail = gbuf[tm:tm + hb, :]
    hout_ref[...] = tail
    gbuf[0:hb, :] = tail


def _ffn_call(x, hist, g, wg, wu, cwt, cbias, wd, *, stride, layer):
    rows = x.shape[0]
    tm = _row_tile(stride)
    hb = (FFN_CONV_WIDTH - 1) * stride
    kern = functools.partial(_ffn_kernel, tm=tm, hb=hb, stride=stride, cw=256)
    return pl.pallas_call(
        kern,
        grid=(rows // tm,),
        in_specs=[
            pl.BlockSpec((tm, D_MODEL), lambda i: (i, 0)),
            _hist_spec(hb, D_FF),
            _resident(), _layer_weight(wg, layer), _layer_weight(wu, layer), _resident(), _resident(),
            _layer_weight(wd, layer),
        ],
        out_specs=[
            pl.BlockSpec((tm, D_MODEL), lambda i: (i, 0)),
            _hist_spec(hb, D_FF),
        ],
        out_shape=[
            jax.ShapeDtypeStruct((rows, D_MODEL), F32),
            jax.ShapeDtypeStruct((hb, D_FF), F32),
        ],
        scratch_shapes=[pltpu.VMEM((hb + tm, D_FF), F32), pltpu.VMEM((tm, D_FF), BF16)],
        compiler_params=_params(1),
        name="conv_ffn",
    )(x, hist, g, wg, wu, cwt, cbias, wd)


CONV_ROWS = 16


def _even_pre_kernel(x_ref, hist_ref, ra_ref, rb_ref, rc_ref, g_ref, win_ref, cw_ref, cb_ref,
                     lng_ref, lnb_ref, qg_ref, kg_ref, pmean_ref,
                     c_ref, q_ref, k_ref, v_ref, hout_ref, ubuf, cbuf, *, tm, hb, stride):
    @pl.when(pl.program_id(0) == 0)
    def _():
        ubuf[0:hb, :] = hist_ref[...]

    x = x_ref[...]
    h = _rms(x, g_ref[...]).astype(BF16)
    proj = _dot(h, _w(win_ref))
    u = proj[:, 0:CONV_A_DIM] * _sigmoid(proj[:, CONV_A_DIM:2 * CONV_A_DIM])
    ubuf[hb:hb + tm, :] = u

    for r0 in range(0, tm, CONV_ROWS):
        acc = jnp.zeros((CONV_ROWS, CONV_A_DIM), F32)
        for j in range(CONV_A_WIDTH):
            off = r0 + j * stride
            acc = acc + cw_ref[j:j + 1, :] * ubuf[off:off + CONV_ROWS, :]
        cbuf[r0:r0 + CONV_ROWS, :] = acc + cb_ref[...]

    conv = cbuf[...]
    mu = jnp.mean(conv, axis=-1, keepdims=True)
    cen = conv - mu
    var = jnp.mean(cen * cen, axis=-1, keepdims=True)
    ln = cen * lax.rsqrt(var + EPS) * lng_ref[...] + lnb_ref[...]
    c_ref[...] = _silu(ln).astype(BF16)

    tail = ubuf[tm:tm + hb, :]
    hout_ref[...] = tail
    ubuf[0:hb, :] = tail

    ra, rb, rc = ra_ref[...], rb_ref[...], rc_ref[...]
    pmean = pmean_ref[...]
    q0 = 2 * CONV_A_DIM
    for s in range(Q_SLABS):
        qs = proj[:, q0 + s * LANES:q0 + (s + 1) * LANES]
        ms = _dot_split(qs * qs, pmean)
        qn = qs * lax.rsqrt(ms + EPS) * qg_ref[...]
        q_ref[s] = _rope(qn, ra, rb, rc)
    ks = proj[:, q0 + QD:q0 + QD + KD]
    ms = _dot_split(ks * ks, pmean)
    kn = ks * lax.rsqrt(ms + EPS) * kg_ref[...]
    k_ref[...] = _rope(kn, ra, rb, rc)
    v_ref[...] = proj[:, q0 + QD + KD:q0 + QD + 2 * KD]


def _even_pre_call(x, hist, ra, rb, rc, g, win, cwt, cbias, lng, lnb, qg, kg, pmean, *, stride, layer):
    rows = x.shape[0]
    tm = _row_tile(stride)
    hb = (CONV_A_WIDTH - 1) * stride
    kern = functools.partial(_even_pre_kernel, tm=tm, hb=hb, stride=stride)
    row_spec = lambda w: pl.BlockSpec((tm, w), lambda i: (i, 0))
    return pl.pallas_call(
        kern,
        grid=(rows // tm,),
        in_specs=[
            row_spec(D_MODEL),
            _hist_spec(hb, CONV_A_DIM),
            row_spec(LANES), row_spec(LANES), row_spec(LANES),
            _resident(), _layer_weight(win, layer),
        ] + [_resident()] * 7,
        out_specs=[
            row_spec(CONV_A_DIM),
            pl.BlockSpec((Q_SLABS, tm, LANES), lambda i: (0, i, 0)),
            row_spec(KD), row_spec(KD),
            _hist_spec(hb, CONV_A_DIM),
        ],
        out_shape=[
            jax.ShapeDtypeStruct((rows, CONV_A_DIM), BF16),
            jax.ShapeDtypeStruct((Q_SLABS, rows, LANES), F32),
            jax.ShapeDtypeStruct((rows, KD), F32),
            jax.ShapeDtypeStruct((rows, KD), F32),
            jax.ShapeDtypeStruct((hb, CONV_A_DIM), F32),
        ],
        scratch_shapes=[pltpu.VMEM((hb + tm, CONV_A_DIM), F32), pltpu.VMEM((tm, CONV_A_DIM), F32)],
        compiler_params=_params(1),
        name="even_pre",
    )(x, hist, ra, rb, rc, g, win, cwt, cbias, lng, lnb, qg, kg, pmean)


def _attn_out_kernel(sink_ref, q_ref, k_ref, v_ref, c_ref, x_ref, wout_ref,
                     out_ref, kprev, vprev, obuf, *, nseq):
    step = pl.program_id(0)

    @pl.when(step == 0)
    def _():
        kprev[...] = jnp.zeros_like(kprev)
        vprev[...] = jnp.zeros_like(vprev)

    row = lax.broadcasted_iota(jnp.int32, (WINDOW, 2 * WINDOW), 0)
    col = lax.broadcasted_iota(jnp.int32, (WINDOW, 2 * WINDOW), 1)
    valid = (col - row >= 1) & (col - row <= WINDOW) & ((col >= WINDOW) | (step > 0))

    def per_seq(b, carry):
        rows = pl.ds(b, WINDOW, stride=nseq)
        kb = k_ref[rows, :].astype(BF16)
        vb = v_ref[rows, :].astype(BF16)
        kk = jnp.concatenate([kprev[b], kb], axis=0)
        vv = jnp.concatenate([vprev[b], vb], axis=0)
        for j in range(Q_SLABS):
            qj = q_ref[j, rows, :].astype(BF16)
            outs = []
            for h2 in range(LANES // HEAD_DIM):
                hd = j * (LANES // HEAD_DIM) + h2
                ls = slice((hd // Q_PER_KV) * HEAD_DIM, (hd // Q_PER_KV + 1) * HEAD_DIM)
                s = _dot_nt(qj[:, h2 * HEAD_DIM:(h2 + 1) * HEAD_DIM], kk[:, ls]) * (HEAD_DIM ** -0.5)
                s = jnp.where(valid, s, -jnp.inf)
                p = _softmax_with_sink(s, sink_ref[hd]).astype(BF16)
                outs.append(_dot(p, vv[:, ls]))
            obuf[j, rows, :] = jnp.concatenate(outs, axis=1)
        kprev[b] = kb
        vprev[b] = vb
        return carry

    lax.fori_loop(0, nseq, per_seq, 0)
    o = _slab_cat(obuf).astype(BF16)
    mixed = (_dot(c_ref[...], _w(wout_ref, rows=slice(0, CONV_A_DIM)))
             + _dot(o, _w(wout_ref, rows=slice(CONV_A_DIM, None))))
    out_ref[...] = x_ref[...] + mixed


def _attn_out_call(sinks, q, k, v, c, x, wout, *, nseq, layer):
    rows = x.shape[0]
    tq = WINDOW * nseq
    kern = functools.partial(_attn_out_kernel, nseq=nseq)
    row_spec = lambda w: pl.BlockSpec((tq, w), lambda i: (i, 0))
    return pl.pallas_call(
        kern,
        grid=(rows // tq,),
        in_specs=[
            pl.BlockSpec(memory_space=pltpu.SMEM),
            pl.BlockSpec((Q_SLABS, tq, LANES), lambda i: (0, i, 0)),
            row_spec(KD), row_spec(KD), row_spec(CONV_A_DIM), row_spec(D_MODEL), _layer_weight(wout, layer),
        ],
        out_specs=row_spec(D_MODEL),
        out_shape=jax.ShapeDtypeStruct((rows, D_MODEL), F32),
        scratch_shapes=[pltpu.VMEM((nseq, WINDOW, KD), BF16), pltpu.VMEM((nseq, WINDOW, KD), BF16),
                        pltpu.VMEM((Q_SLABS, tq, LANES), F32)],
        compiler_params=_params(1),
        name="attn_out",
    )(sinks, q, k, v, c, x, wout)


def _sample_attn_kernel(sink_ref, q_ref, kn_ref, vn_ref, kh_ref, vh_ref, o_ref, *, t):
    rows = Q_PER_KV * t
    keys = WINDOW + t
    r = lax.broadcasted_iota(jnp.int32, (rows, keys), 0)
    j = lax.broadcasted_iota(jnp.int32, (rows, keys), 1)
    tq = r % t
    valid = (j > tq) & (j <= tq + WINDOW)
    grp = lax.broadcasted_iota(jnp.int32, (rows, 1), 0) // t
    for b in range(SAMPLE_SEQ_BLOCK):
        kall = jnp.concatenate([kh_ref[b], kn_ref[b]], axis=0).astype(BF16)
        vall = jnp.concatenate([vh_ref[b], vn_ref[b]], axis=0).astype(BF16)
        for kv in range(N_KV_HEADS):
            sink = jnp.zeros((rows, 1), F32)
            for g in range(Q_PER_KV):
                sink = jnp.where(grp == g, sink_ref[kv * Q_PER_KV + g], sink)
            ls = slice(kv * HEAD_DIM, (kv + 1) * HEAD_DIM)
            s = _dot_nt(q_ref[b, kv].astype(BF16), kall[:, ls]) * (HEAD_DIM ** -0.5)
            s = jnp.where(valid, s, -jnp.inf)
            p = _softmax_with_sink(s, sink).astype(BF16)
            o_ref[b, kv] = _dot(p, vall[:, ls])


def _sample_attn_call(sinks, q, kn, vn, kh, vh, layer):
    nb, _, rows, _ = q.shape
    t = kn.shape[1]
    bb = SAMPLE_SEQ_BLOCK
    kern = functools.partial(_sample_attn_kernel, t=t)
    hist_spec = pl.BlockSpec((None, bb, WINDOW, KD), lambda i: (layer, i, 0, 0))
    return pl.pallas_call(
        kern,
        grid=(nb // bb,),
        in_specs=[
            pl.BlockSpec(memory_space=pltpu.SMEM),
            pl.BlockSpec((bb, N_KV_HEADS, rows, HEAD_DIM), lambda i: (i, 0, 0, 0)),
            pl.BlockSpec((bb, t, KD), lambda i: (i, 0, 0)),
            pl.BlockSpec((bb, t, KD), lambda i: (i, 0, 0)),
            hist_spec, hist_spec,
        ],
        out_specs=pl.BlockSpec((bb, N_KV_HEADS, rows, HEAD_DIM), lambda i: (i, 0, 0, 0)),
        out_shape=jax.ShapeDtypeStruct((nb, N_KV_HEADS, rows, HEAD_DIM), F32),
        compiler_params=_params(1),
        name="sample_attn",
    )(sinks, q, kn, vn, kh, vh)


def _even_out_kernel(c_ref, o_ref, x_ref, wout_ref, out_ref):
    mixed = (_dot(c_ref[...], _w(wout_ref, rows=slice(0, CONV_A_DIM)))
             + _dot(o_ref[...].astype(BF16), _w(wout_ref, rows=slice(CONV_A_DIM, None))))
    out_ref[...] = x_ref[...] + mixed


def _even_out_call(c, o, x, wout, *, layer):
    rows = x.shape[0]
    tm = ROW_TILE
    row_spec = lambda w: pl.BlockSpec((tm, w), lambda i: (i, 0))
    return pl.pallas_call(
        _even_out_kernel,
        grid=(rows // tm,),
        in_specs=[row_spec(CONV_A_DIM), row_spec(QD), row_spec(D_MODEL), _layer_weight(wout, layer)],
        out_specs=row_spec(D_MODEL),
        out_shape=jax.ShapeDtypeStruct((rows, D_MODEL), F32),
        compiler_params=_params(1),
        name="even_out",
    )(c, o, x, wout)


def _odd_pre_kernel(x_ref, hist_ref, g_ref, win_ref, cw_ref, cb_ref, dtb_ref,
                    z_ref, xs_ref, b_ref, c_ref, dt_ref, hout_ref, xbuf, *, tm, hb, stride, cw):
    @pl.when(pl.program_id(0) == 0)
    def _():
        xbuf[0:hb, :] = hist_ref[...]

    h = _rms(x_ref[...], g_ref[...]).astype(BF16)
    for j in range(D_INNER // cw):
        cs = slice(j * cw, (j + 1) * cw)
        z_ref[:, cs] = _dot(h, _w(win_ref, cols=cs))
    spb = cw // LANES
    for j in range(SSM_CONV_DIM // cw):
        cs = slice(j * cw, (j + 1) * cw)
        pre = _dot(h, _w(win_ref, cols=slice(D_INNER + j * cw, D_INNER + (j + 1) * cw)))
        xbuf[hb:hb + tm, cs] = pre
        conv = cw_ref[SSM_CONV_WIDTH - 1:SSM_CONV_WIDTH, cs] * pre + cb_ref[:, cs]
        for tap in range(SSM_CONV_WIDTH - 1):
            off = hb - (SSM_CONV_WIDTH - 1 - tap) * stride
            conv = conv + cw_ref[tap:tap + 1, cs] * xbuf[off:off + tm, cs]
        act = _silu(conv)
        for q in range(spb):
            slab = j * spb + q
            piece = act[:, q * LANES:(q + 1) * LANES]
            if slab < X_SLABS:
                xs_ref[slab] = piece
            elif slab < X_SLABS + G_SLABS:
                b_ref[slab - X_SLABS] = piece
            else:
                c_ref[slab - X_SLABS - G_SLABS] = piece
    dt_cols = slice(D_INNER + SSM_CONV_DIM, D_INNER + SSM_CONV_DIM + SSM_HEADS)
    dt_ref[...] = _softplus(_dot(h, _w(win_ref, cols=dt_cols)) + dtb_ref[...])
    tail = xbuf[tm:tm + hb, :]
    hout_ref[...] = tail
    xbuf[0:hb, :] = tail


def _odd_pre_call(x, hist, g, win, cwt, cbias, dtb, *, stride, layer):
    rows = x.shape[0]
    tm = _row_tile(stride)
    hb = (SSM_CONV_WIDTH - 1) * stride
    kern = functools.partial(_odd_pre_kernel, tm=tm, hb=hb, stride=stride, cw=512)
    row_spec = lambda w: pl.BlockSpec((tm, w), lambda i: (i, 0))
    slab_spec = lambda n: pl.BlockSpec((n, tm, LANES), lambda i: (0, i, 0))
    return pl.pallas_call(
        kern,
        grid=(rows // tm,),
        in_specs=[
            row_spec(D_MODEL),
            _hist_spec(hb, SSM_CONV_DIM),
            _resident(), _layer_weight(win, layer), _resident(), _resident(), _resident(),
        ],
        out_specs=[
            row_spec(D_INNER), slab_spec(X_SLABS), slab_spec(G_SLABS), slab_spec(G_SLABS), row_spec(SSM_HEADS),
            _hist_spec(hb, SSM_CONV_DIM),
        ],
        out_shape=[
            jax.ShapeDtypeStruct((rows, D_INNER), F32),
            jax.ShapeDtypeStruct((X_SLABS, rows, LANES), F32),
            jax.ShapeDtypeStruct((G_SLABS, rows, LANES), F32),
            jax.ShapeDtypeStruct((G_SLABS, rows, LANES), F32),
            jax.ShapeDtypeStruct((rows, SSM_HEADS), F32),
            jax.ShapeDtypeStruct((hb, SSM_CONV_DIM), F32),
        ],
        scratch_shapes=[pltpu.VMEM((hb + tm, SSM_CONV_DIM), F32)],
        compiler_params=_params(1),
        name="odd_pre",
    )(x, hist, g, win, cwt, cbias, dtb)


def _ssd_kernel(*refs, nseq, has_prev):
    (xs_ref, b_ref, c_ref, dt_ref, alog_ref, alogt_ref, tril_ref, triu_ref, expand_ref, dskip_ref) = refs[:10]
    y_ref, ht = refs[10 + has_prev], refs[11 + has_prev]
    dtbuf = refs[12 + has_prev]
    L = SSD_CHUNK
    N = SSM_STATE
    step = pl.program_id(0)

    @pl.when(step == 0)
    def _():
        ht[...] = jnp.zeros_like(ht)
        dtbuf[...] = jnp.zeros_like(dtbuf)

    dtbuf[:, 0:SSM_HEADS] = dt_ref[...]
    a_row = -jnp.exp(alog_ref[...])
    a_col = -jnp.exp(alogt_ref[...])
    tril, triu, expand = tril_ref[...], triu_ref[...], expand_ref[...]
    row = lax.broadcasted_iota(jnp.int32, (L, L), 0)
    col = lax.broadcasted_iota(jnp.int32, (L, L), 1)
    causal = row >= col
    lane = lax.broadcasted_iota(jnp.int32, (L, LANES), 1)
    low_half = lane < SSM_HEAD_DIM

    def per_seq(b, carry):
        rows = pl.ds(b, L, stride=nseq)
        dt_wide = dtbuf[rows, :]
        dt = dt_wide[:, 0:SSM_HEADS]
        dtt = dt_wide.T[0:SSM_HEADS, :]
        acum = _dot_f32(tril, dt * a_row)
        acumt = _dot_f32(dtt * a_col, triu)
        last = acum[L - 1:L, :]
        ea = _dot_split(jnp.exp(acum), expand)
        de = _dot_split(jnp.exp(last - acum) * dt, expand)
        cd = _dot_split(jnp.broadcast_to(jnp.exp(last), (SUBLANES, SSM_HEADS)), expand)[0:1, :]

        cbs, y_off, b_t = [], [], []
        for g in range(SSM_GROUPS):
            cg = c_ref[g, rows, :].astype(BF16)
            bg = b_ref[g, rows, :]
            cbs.append(_dot_nt(cg, bg.astype(BF16)))
            y_off.append(_dot(cg, ht[b, g * N:(g + 1) * N, :].astype(BF16)))
            b_t.append(bg.T.astype(BF16))

        xw = []
        for slab in range(X_SLABS):
            g = slab // SLABS_PER_GROUP
            ws = []
            for hd in (2 * slab, 2 * slab + 1):
                diff = jnp.where(causal, acum[:, hd:hd + 1] - acumt[hd:hd + 1, :], -jnp.inf)
                ws.append(cbs[g] * jnp.exp(diff) * dtt[hd:hd + 1, :])
            wcat = jnp.concatenate(ws, axis=1).astype(BF16)
            xpair = xs_ref[slab, rows, :]
            xbd = jnp.concatenate([jnp.where(low_half, xpair, 0.0), jnp.where(low_half, 0.0, xpair)], axis=0)
            ls = slice(slab * LANES, (slab + 1) * LANES)
            gl = slice((slab % SLABS_PER_GROUP) * LANES, (slab % SLABS_PER_GROUP + 1) * LANES)
            y_ref[slab, rows, :] = (_dot(wcat, xbd.astype(BF16)) + ea[:, ls] * y_off[g][:, gl]
                                    + dskip_ref[:, ls] * xpair)
            xw.append((xpair * de[:, ls]).astype(BF16))

        for g in range(SSM_GROUPS):
            xw_g = jnp.concatenate(xw[g * SLABS_PER_GROUP:(g + 1) * SLABS_PER_GROUP], axis=1)
            gs = slice(g * N, (g + 1) * N)
            ht[b, gs, :] = cd[:, g * GROUP_WIDTH:(g + 1) * GROUP_WIDTH] * ht[b, gs, :] + _dot(b_t[g], xw_g)
        return carry

    lax.fori_loop(0, nseq, per_seq, 0)


def _ssd_call(xs, bm, cm, dt, alog, alogt, tril, triu, expand, dskip, prev_state, *, nseq, layer, n_layers):
    rows = dt.shape[0]
    tq = SSD_CHUNK * nseq
    has_prev = prev_state is not None
    kern = functools.partial(_ssd_kernel, nseq=nseq, has_prev=int(has_prev))
    slab_spec = lambda n: pl.BlockSpec((n, tq, LANES), lambda i: (0, i, 0))
    in_specs = [
        pl.BlockSpec((X_SLABS, tq, LANES), lambda i: (0, i, 0), pipeline_mode=pl.Buffered(1)),
        slab_spec(G_SLABS), slab_spec(G_SLABS),
        pl.BlockSpec((tq, SSM_HEADS), lambda i: (i, 0)),
    ] + [_resident()] * 6
    args = [xs, bm, cm, dt, alog, alogt, tril, triu, expand, dskip]
    aliases = {}
    if has_prev:
        in_specs.append(pl.BlockSpec(memory_space=pl.ANY))
        args.append(prev_state)
        aliases = {len(args) - 1: 1}
    return pl.pallas_call(
        kern,
        grid=(rows // tq,),
        in_specs=in_specs,
        out_specs=[
            slab_spec(X_SLABS),
            pl.BlockSpec((None, nseq, GN, GROUP_WIDTH), lambda i: (layer, 0, 0, 0)),
        ],
        out_shape=[
            jax.ShapeDtypeStruct((X_SLABS, rows, LANES), F32),
            jax.ShapeDtypeStruct((n_layers, nseq, GN, GROUP_WIDTH), F32),
        ],
        scratch_shapes=[pltpu.VMEM((tq, LANES), F32)],
        input_output_aliases=aliases,
        compiler_params=_params(1),
        name="ssd_chunk",
    )(*args)


def _sample_ssd_intra_kernel(xs_ref, b_ref, c_ref, dt_ref, alog_ref, expand_ref, gexp_ref, dskip_ref,
                             ydiag_ref, xw_ref, ea_ref, cd_ref, *, t, nb):
    a_row = -jnp.exp(alog_ref[...])
    expand = expand_ref[...]
    gexp = gexp_ref[...]
    blk = lambda l: slice(l * nb, (l + 1) * nb)
    dts = [dt_ref[blk(l), :] for l in range(t)]
    acum = []
    run = None
    for l in range(t):
        step = dts[l] * a_row
        run = step if run is None else run + step
        acum.append(run)
    last = acum[t - 1]
    cd_ref[...] = _dot_split(jnp.exp(last), expand)
    for l in range(t):
        ea_ref[blk(l), :] = _dot_split(jnp.exp(acum[l]), expand)
        xs_l = _slab_cat(xs_ref, blk(l))
        xw_ref[blk(l), :] = xs_l * _dot_split(jnp.exp(last - acum[l]) * dts[l], expand)
        yd = dskip_ref[...] * xs_l
        c_l = _slab_cat(c_ref, blk(l))
        for s in range(l + 1):
            cbx = _dot((c_l * _slab_cat(b_ref, blk(s))).astype(BF16), gexp)
            w = _dot_split(jnp.exp(acum[l] - acum[s]) * dts[s], expand)
            yd = yd + cbx * w * _slab_cat(xs_ref, blk(s))
        ydiag_ref[blk(l), :] = yd


def _sample_ssd_intra_call(xs, bm, cm, dt, alog, expand, gexp, dskip, *, t, nb):
    rows = dt.shape[0]
    kern = functools.partial(_sample_ssd_intra_kernel, t=t, nb=nb)
    return pl.pallas_call(
        kern,
        in_specs=[_resident()] * 8,
        out_specs=[_resident()] * 4,
        out_shape=[
            jax.ShapeDtypeStruct((rows, D_INNER), F32),
            jax.ShapeDtypeStruct((rows, D_INNER), F32),
            jax.ShapeDtypeStruct((rows, D_INNER), F32),
            jax.ShapeDtypeStruct((nb, D_INNER), F32),
        ],
        compiler_params=pltpu.CompilerParams(vmem_limit_bytes=V7X_VMEM_LIMIT_BYTES),
        name="sample_ssd_intra",
    )(xs, bm, cm, dt, alog, expand, gexp, dskip)


def _sample_ssd_state_kernel(*refs, has_prev):
    h0_ref, c_ref, b_ref, xw_ref, ydiag_ref, ea_ref, cd_ref = refs[:7]
    y_ref, hnew_ref = refs[7 + has_prev], refs[8 + has_prev]
    for j in range(SAMPLE_SEQ_BLOCK):
        ht = h0_ref[j].T
        xw = xw_ref[:, j, :].astype(BF16)
        y_parts, new_parts = [], []
        for g in range(SSM_GROUPS):
            ws_ = slice(g * GROUP_WIDTH, (g + 1) * GROUP_WIDTH)
            y_parts.append(_dot(c_ref[g, :, j, :].astype(BF16), ht[:, ws_].astype(BF16)))
            new_parts.append(_dot_tn(b_ref[g, :, j, :].astype(BF16), xw[:, ws_]))
        y_ref[:, j, :] = ydiag_ref[:, j, :] + ea_ref[:, j, :] * jnp.concatenate(y_parts, axis=1)
        hnew = cd_ref[j:j + 1, :] * ht + jnp.concatenate(new_parts, axis=1)
        hnew_ref[j] = hnew.T


def _sample_ssd_state_call(h0_all, cm, bm, xw, ydiag, ea, cd, prev_state, *, layer):
    n_layers, nb = h0_all.shape[:2]
    t = xw.shape[0]
    bb = SAMPLE_SEQ_BLOCK
    has_prev = prev_state is not None
    kern = functools.partial(_sample_ssd_state_kernel, has_prev=int(has_prev))
    tspec = lambda w: pl.BlockSpec((t, bb, w), lambda i: (0, i, 0))
    gspec = pl.BlockSpec((G_SLABS, t, bb, LANES), lambda i: (0, 0, i, 0))
    state_spec = pl.BlockSpec((None, bb, D_INNER, SSM_STATE), lambda i: (layer, i, 0, 0))
    in_specs = [state_spec, gspec, gspec, tspec(D_INNER), tspec(D_INNER), tspec(D_INNER),
                pl.BlockSpec((bb, D_INNER), lambda i: (i, 0))]
    args = [h0_all, cm, bm, xw, ydiag, ea, cd]
    aliases = {}
    if has_prev:
        in_specs.append(pl.BlockSpec(memory_space=pl.ANY))
        args.append(prev_state)
        aliases = {len(args) - 1: 1}
    return pl.pallas_call(
        kern,
        grid=(nb // bb,),
        in_specs=in_specs,
        out_specs=[tspec(D_INNER), state_spec],
        out_shape=[
            jax.ShapeDtypeStruct((t, nb, D_INNER), F32),
            jax.ShapeDtypeStruct((n_layers, nb, D_INNER, SSM_STATE), F32),
        ],
        input_output_aliases=aliases,
        compiler_params=_params(1),
        name="sample_ssd_state",
    )(*args)


def _odd_out_kernel(y_ref, z_ref, x_ref, gn_ref, wout_ref, out_ref, *, y_slabs):
    y = _slab_cat(y_ref) if y_slabs else y_ref[...]
    v = y * _silu(z_ref[...])
    parts = []
    for g in range(SSM_GROUPS):
        vg = v[:, g * GROUP_WIDTH:(g + 1) * GROUP_WIDTH]
        ms = jnp.mean(vg * vg, axis=-1, keepdims=True)
        parts.append(vg * lax.rsqrt(ms + EPS))
    vn = (jnp.concatenate(parts, axis=1) * gn_ref[...]).astype(BF16)
    out_ref[...] = x_ref[...] + _dot(vn, _w(wout_ref))


def _odd_out_call(y, z, x, gn, wout, *, layer):
    rows = x.shape[0]
    tm = ROW_TILE
    y_slabs = y.ndim == 3
    row_spec = lambda w: pl.BlockSpec((tm, w), lambda i: (i, 0))
    slab_spec = pl.BlockSpec((X_SLABS, tm, LANES), lambda i: (0, i, 0))
    return pl.pallas_call(
        functools.partial(_odd_out_kernel, y_slabs=y_slabs),
        grid=(rows // tm,),
        in_specs=[slab_spec if y_slabs else row_spec(D_INNER), row_spec(D_INNER), row_spec(D_MODEL),
                  _resident(), _layer_weight(wout, layer)],
        out_specs=row_spec(D_MODEL),
        out_shape=jax.ShapeDtypeStruct((rows, D_MODEL), F32),
        compiler_params=_params(1),
        name="odd_out",
    )(y, z, x, gn, wout)


def _rope_tables(pos):
    half = ROPE_DIM // 2
    inv_freq = ROPE_THETA ** (-jnp.arange(0, ROPE_DIM, 2, dtype=F32) / ROPE_DIM)
    ang = pos.astype(F32)[:, None] * inv_freq[None, :]
    cos, sin = jnp.cos(ang), jnp.sin(ang)
    n = pos.shape[0]
    rest = HEAD_DIM - ROPE_DIM
    ra = jnp.concatenate([cos, cos, jnp.ones((n, rest), F32)], axis=1)
    rb = jnp.concatenate([jnp.zeros((n, half), F32), sin, jnp.zeros((n, rest), F32)], axis=1)
    rc = jnp.concatenate([-sin, jnp.zeros((n, half + rest), F32)], axis=1)
    rep = LANES // HEAD_DIM
    return tuple(jnp.tile(m, (1, rep)) for m in (ra, rb, rc))


def _head_mean_matrix():
    idx = jnp.arange(LANES) // HEAD_DIM
    return ((idx[:, None] == idx[None, :]).astype(F32) / HEAD_DIM).astype(BF16)


def _head_expand_matrix():
    idx = jnp.arange(D_INNER) // SSM_HEAD_DIM
    return (jnp.arange(SSM_HEADS)[:, None] == idx[None, :]).astype(BF16)


def _group_expand_matrix():
    src = jnp.arange(GN) // SSM_STATE
    dst = jnp.arange(D_INNER) // GROUP_WIDTH
    return (src[:, None] == dst[None, :]).astype(BF16)


def _make_tables(tp, bp, ts, nb):
    tril = jnp.tril(jnp.ones((SSD_CHUNK, SSD_CHUNK), F32))
    return {
        "rope_p": _rope_tables(jnp.repeat(jnp.arange(tp, dtype=jnp.int32), bp)),
        "rope_s": _rope_tables(PAST_LEN + jnp.repeat(jnp.arange(ts, dtype=jnp.int32), nb)),
        "pmean": _head_mean_matrix(),
        "expand": _head_expand_matrix(),
        "gexp": _group_expand_matrix(),
        "tril": tril,
        "triu": tril.T,
    }


def _row(v):
    return v.reshape(1, -1)


def _time_major_hist(state):
    return jnp.swapaxes(state, 0, 1).reshape(-1, state.shape[-1])


def _batch_major_hist(rows, nseq):
    return jnp.swapaxes(rows.reshape(-1, nseq, rows.shape[-1]), 0, 1)


def _even_layer(xp, xs, conv_hist_s, k_hist_all, v_hist_all, layer, norm_g, w_in, conv_w, conv_b, ln_g, ln_b,
                qn_g, kn_g, sinks, w_out, tables, *, bp, tp, nb, ts):
    shared = (_row(norm_g), w_in, conv_w, _row(conv_b), _row(ln_g), _row(ln_b),
              _row(jnp.tile(qn_g, LANES // HEAD_DIM)), _row(jnp.tile(kn_g, LANES // HEAD_DIM)), tables["pmean"])
    keep = CONV_A_WIDTH - 1
    wout = w_out

    c, q, k, v, hout = _even_pre_call(xp, jnp.zeros((keep * bp, CONV_A_DIM), F32), *tables["rope_p"], *shared,
                                      stride=bp, layer=layer)
    xp_new = _attn_out_call(sinks, q, k, v, c, xp, wout, nseq=bp, layer=layer)
    conv_p = _batch_major_hist(hout, bp)
    last = lambda a: _batch_major_hist(a[-WINDOW * bp:], bp).reshape(bp, WINDOW, N_KV_HEADS, HEAD_DIM)
    wk_p, wv_p = last(k), last(v)

    c, q, k, v, hout = _even_pre_call(xs, _time_major_hist(conv_hist_s), *tables["rope_s"], *shared, stride=nb,
                                      layer=layer)
    conv_s = _batch_major_hist(hout, nb)
    k_new = _batch_major_hist(k, nb)
    v_new = _batch_major_hist(v, nb)
    per_slab = LANES // HEAD_DIM
    q_b = q.reshape(N_KV_HEADS, Q_SLABS // N_KV_HEADS, ts, nb, per_slab, HEAD_DIM).transpose(3, 0, 1, 4, 2, 5)
    q_b = q_b.reshape(nb, N_KV_HEADS, Q_PER_KV * ts, HEAD_DIM)
    kh_all = k_hist_all.reshape(-1, nb, WINDOW, KD)
    vh_all = v_hist_all.reshape(-1, nb, WINDOW, KD)
    o = _sample_attn_call(sinks, q_b, k_new, v_new, kh_all, vh_all, layer)
    o = o.reshape(nb, N_KV_HEADS, Q_PER_KV, ts, HEAD_DIM).transpose(3, 0, 1, 2, 4).reshape(ts * nb, QD)
    xs_new = _even_out_call(c, o, xs, wout, layer=layer)
    win_s = lambda hist, new: jnp.concatenate([hist[layer], new], axis=1)[:, -WINDOW:].reshape(
        nb, WINDOW, N_KV_HEADS, HEAD_DIM)
    return xp_new, xs_new, conv_p, conv_s, wk_p, win_s(kh_all, k_new), wv_p, win_s(vh_all, v_new)


def _odd_layer(xp, xs, conv_hist_s, ssm_all, layer, state_p, state_s, norm_g, w_in, conv_w, conv_b, dt_bias, a_log,
               d_skip, gn_g, w_out, tables, *, bp, tp, nb, ts):
    dskip = _row(jnp.repeat(d_skip, SSM_HEAD_DIM))
    gn = _row(gn_g)
    alog = _row(a_log)
    pre_shared = (_row(norm_g), w_in, conv_w, _row(conv_b), _row(dt_bias))
    keep = SSM_CONV_WIDTH - 1
    n_layers = ssm_all.shape[0]

    z, xc, bm, cm, dt, hout = _odd_pre_call(xp, jnp.zeros((keep * bp, SSM_CONV_DIM), F32), *pre_shared, stride=bp,
                                            layer=layer)
    y, state_p = _ssd_call(xc, bm, cm, dt, alog, a_log.reshape(-1, 1), tables["tril"], tables["triu"],
                           tables["expand"], dskip, state_p, nseq=bp, layer=layer, n_layers=n_layers)
    xp_new = _odd_out_call(y, z, xp, gn, w_out, layer=layer)
    conv_p = _batch_major_hist(hout, bp)

    z, xc, bm, cm, dt, hout = _odd_pre_call(xs, _time_major_hist(conv_hist_s), *pre_shared, stride=nb, layer=layer)
    conv_s = _batch_major_hist(hout, nb)
    ydiag, xw, ea, cd = _sample_ssd_intra_call(xc, bm, cm, dt, alog, tables["expand"], tables["gexp"], dskip,
                                               t=ts, nb=nb)
    tm3 = lambda a: a.reshape(ts, nb, a.shape[-1])
    slab4 = lambda a: a.reshape(G_SLABS, ts, nb, LANES)
    y, state_s = _sample_ssd_state_call(ssm_all, slab4(cm), slab4(bm), tm3(xw), tm3(ydiag), tm3(ea), cd, state_s,
                                        layer=layer)
    xs_new = _odd_out_call(y.reshape(ts * nb, D_INNER), z, xs, gn, w_out, layer=layer)
    return xp_new, xs_new, conv_p, conv_s, state_p, state_s


def _ffn_layer(xp, xs, hist_s, layer, norm_g, w_gate, w_up, conv_w, conv_b, w_down, *, bp, nb):
    keep = FFN_CONV_WIDTH - 1
    shared = (_row(norm_g), w_gate, w_up, conv_w, _row(conv_b), w_down)
    xp_new, hout = _ffn_call(xp, jnp.zeros((keep * bp, D_FF), F32), *shared, stride=bp, layer=layer)
    ffn_p = _batch_major_hist(hout, bp)
    xs_new, hout = _ffn_call(xs, _time_major_hist(hist_s), *shared, stride=nb, layer=layer)
    ffn_s = _batch_major_hist(hout, nb)
    return xp_new, xs_new, ffn_p, ffn_s


def kernel(x_prompt, x_sample, state_conv_a, cache_win_k, cache_win_v, state_conv_c, state_ssm, state_ffn_conv,
           norm_mix_e, w_in_e, conv_a_w, conv_a_b, ln_a_g, ln_a_b, q_norm_g, k_norm_g, sinks, w_out_e,
           norm_mix_o, w_in_o, conv_c_w, conv_c_b, dt_bias, a_log, d_skip, gnorm_c, w_out_o,
           norm_ffn, w_gate, w_up, ffn_conv_w, ffn_conv_b, w_down):
    bp, tp, _ = x_prompt.shape
    nb, ts, _ = x_sample.shape
    depth = norm_ffn.shape[0]
    n_odd = state_ssm.shape[0]
    tables = _make_tables(tp, bp, ts, nb)
    dims = dict(bp=bp, tp=tp, nb=nb, ts=ts)

    xp = jnp.swapaxes(x_prompt, 0, 1).reshape(tp * bp, D_MODEL)
    xs = jnp.swapaxes(x_sample, 0, 1).reshape(ts * nb, D_MODEL)
    ssm_all = state_ssm.reshape(n_odd, nb, D_INNER, SSM_STATE)
    state_p = state_s = None
    ca_p, ca_s, wk_p, wk_s, wv_p, wv_s = [], [], [], [], [], []
    cc_p, cc_s, ff_p, ff_s = [], [], [], []
    for layer in range(depth):
        i = layer // 2
        if layer % 2 == 0:
            xp, xs, c1, c2, k1, k2, v1, v2 = _even_layer(
                xp, xs, state_conv_a[i], cache_win_k, cache_win_v, i, norm_mix_e[i], w_in_e, conv_a_w[i],
                conv_a_b[i], ln_a_g[i], ln_a_b[i], q_norm_g[i], k_norm_g[i], sinks[i], w_out_e, tables, **dims)
            ca_p.append(c1); ca_s.append(c2); wk_p.append(k1); wk_s.append(k2); wv_p.append(v1); wv_s.append(v2)
        else:
            xp, xs, c1, c2, state_p, state_s = _odd_layer(
                xp, xs, state_conv_c[i], ssm_all, i, state_p, state_s, norm_mix_o[i], w_in_o, conv_c_w[i],
                conv_c_b[i], dt_bias[i], a_log[i], d_skip[i], gnorm_c[i], w_out_o, tables, **dims)
            cc_p.append(c1); cc_s.append(c2)
        xp, xs, f1, f2 = _ffn_layer(xp, xs, state_ffn_conv[layer], layer, norm_ffn[layer], w_gate, w_up,
                                    ffn_conv_w[layer], ffn_conv_b[layer], w_down, bp=bp, nb=nb)
        ff_p.append(f1); ff_s.append(f2)

    y_prompt = jnp.swapaxes(xp.reshape(tp, bp, D_MODEL), 0, 1)
    y_sample = jnp.swapaxes(xs.reshape(ts, nb, D_MODEL), 0, 1)
    ssm_p = state_p.reshape(n_odd, bp, SSM_GROUPS, SSM_STATE, HEADS_PER_GROUP, SSM_HEAD_DIM)
    ssm_p = ssm_p.transpose(0, 1, 2, 4, 5, 3).reshape(n_odd, bp, SSM_HEADS, SSM_HEAD_DIM, SSM_STATE)
    ssm_s = state_s.reshape(n_odd, nb, SSM_HEADS, SSM_HEAD_DIM, SSM_STATE)
    return (y_prompt, y_sample,
            jnp.stack(ca_p), jnp.stack(ca_s),
            jnp.stack(wk_p), jnp.stack(wk_s),
            jnp.stack(wv_p), jnp.stack(wv_s),
            jnp.stack(cc_p), jnp.stack(cc_s),
            ssm_p, ssm_s,
            jnp.stack(ff_p), jnp.stack(ff_s))
```

```python
import functools

import jax
import jax.numpy as jnp
from jax import lax
from jax.experimental import pallas as pl
from jax.experimental.pallas import tpu as pltpu

F32 = jnp.float32
BF16 = jnp.bfloat16

D_MODEL = 1024
PAST_LEN = 8192
CONV_A_DIM = 512
CONV_A_WIDTH = 31
HEAD_DIM = 64
N_Q_HEADS = 8
N_KV_HEADS = 2
Q_PER_KV = N_Q_HEADS // N_KV_HEADS
QD = N_Q_HEADS * HEAD_DIM
KD = N_KV_HEADS * HEAD_DIM
WINDOW = 128
ROPE_DIM = 16
ROPE_THETA = 500000.0
D_INNER = 2048
SSM_HEAD_DIM = 64
SSM_HEADS = 32
SSM_STATE = 128
SSM_GROUPS = 4
HEADS_PER_GROUP = SSM_HEADS // SSM_GROUPS
GROUP_WIDTH = D_INNER // SSM_GROUPS
GN = SSM_GROUPS * SSM_STATE
SSM_CONV_WIDTH = 4
SSM_CONV_DIM = D_INNER + 2 * GN
SSD_CHUNK = 128
D_FF = 2816
FFN_CONV_WIDTH = 3
EPS = 1e-6

V7X_VMEM_LIMIT_BYTES = 58 * 1024 * 1024
SUBLANES = 8
LANES = 128
Q_SLABS = QD // LANES
X_SLABS = D_INNER // LANES
G_SLABS = GN // LANES
SLABS_PER_GROUP = GROUP_WIDTH // LANES

ROW_TILE = 512
SAMPLE_SEQ_BLOCK = 8


def _params(n_grid_dims):
    return pltpu.CompilerParams(
        dimension_semantics=("arbitrary",) * n_grid_dims,
        vmem_limit_bytes=V7X_VMEM_LIMIT_BYTES,
    )


def _resident():
    return pl.BlockSpec(memory_space=pltpu.VMEM)


def _layer_weight(stacked, layer):
    _, k, n = stacked.shape
    return pl.BlockSpec((None, k, n), lambda i: (layer, 0, 0), pipeline_mode=pl.Buffered(1))


def _w(ref, rows=slice(None), cols=slice(None)):
    return ref[rows, cols].astype(BF16)


def _row_tile(stride):
    return ROW_TILE if stride <= SUBLANES else ROW_TILE // 2


def _hist_spec(hb, width):
    return pl.BlockSpec((hb, width), lambda i: (0, 0), pipeline_mode=pl.Buffered(1))


def _rms(x, g_row):
    ms = jnp.mean(x * x, axis=-1, keepdims=True)
    return (x * lax.rsqrt(ms + EPS)) * g_row


def _sigmoid(x):
    return 1.0 / (1.0 + jnp.exp(-x))


def _silu(x):
    return x * _sigmoid(x)


def _softplus(x):
    return jnp.maximum(x, 0.0) + jnp.log(1.0 + jnp.exp(-jnp.abs(x)))


def _dot(a, b):
    return jnp.dot(a, b, preferred_element_type=F32)


def _dot_f32(a, b):
    return jnp.dot(a, b, preferred_element_type=F32, precision=lax.Precision.HIGHEST)


def _dot_nt(a, b):
    return lax.dot_general(a, b, (((1,), (1,)), ((), ())), preferred_element_type=F32)


def _dot_tn(a, b):
    return lax.dot_general(a, b, (((0,), (0,)), ((), ())), preferred_element_type=F32)


def _dot_split(a, m_bf16):
    hi = a.astype(BF16)
    lo = (a - hi.astype(F32)).astype(BF16)
    return _dot(hi, m_bf16) + _dot(lo, m_bf16)


def _rope(x, ra, rb, rc):
    return x * ra + pltpu.roll(x, 8, 1) * rb + pltpu.roll(x, LANES - 8, 1) * rc


def _softmax_with_sink(s, sink):
    m = jnp.maximum(jnp.max(s, axis=-1, keepdims=True), sink)
    p = jnp.exp(s - m)
    den = jnp.sum(p, axis=-1, keepdims=True) + jnp.exp(sink - m)
    return p / den


def _slab_cat(ref, rows=slice(None)):
    return jnp.concatenate([ref[j, rows, :] for j in range(ref.shape[0])], axis=1)


def _cast_kernel(src_ref, dst_ref):
    dst_ref[...] = src_ref[...].astype(dst_ref.dtype)


def _cast_bf16_call(stacked):
    n_layers, k, n = stacked.shape
    kb = k // 4
    spec = pl.BlockSpec((None, kb, n), lambda l, i: (l, i, 0))
    return pl.pallas_call(
        _cast_kernel,
        grid=(n_layers, k // kb),
        in_specs=[spec],
        out_specs=spec,
        out_shape=jax.ShapeDtypeStruct(stacked.shape, BF16),
        compiler_params=_params(2),
        name="cast_bf16",
    )(stacked)


def _ffn_kernel(x_ref, hist_ref, g_ref, wg_ref, wu_ref, cw_ref, cb_ref, wd_ref,
                out_ref, hout_ref, gbuf, act, *scratch, tm, hb, stride, cw, out_seqs):
    @pl.when(pl.program_id(0) == 0)
    def _():
        gbuf[0:hb, :] = hist_ref[...]

    x = x_ref[...]
    h = _rms(x, g_ref[...]).astype(BF16)
    for j in range(D_FF // cw):
        cs = slice(j * cw, (j + 1) * cw)
        gate = _dot(h, _w(wg_ref, cols=cs))
        up = _dot(h, _w(wu_ref, cols=cs))
        gbuf[hb:hb + tm, cs] = gate
        conv = (cw_ref[0:1, cs] * gbuf[hb - 2 * stride:hb - 2 * stride + tm, cs]
                + cw_ref[1:2, cs] * gbuf[hb - stride:hb - stride + tm, cs]
                + cw_ref[2:3, cs] * gate + cb_ref[:, cs])
        act[:, cs] = (_silu(conv) * up).astype(BF16)
    y = x + _dot(act[...], _w(wd_ref))
    if out_seqs is None:
        out_ref[...] = y
    else:
        (yslab,) = scratch
        for j in range(D_MODEL // LANES):
            yslab[j] = y[:, j * LANES:(j + 1) * LANES]
        for b in range(out_seqs):
            out_ref[b] = _slab_cat(yslab, pl.ds(b, tm // out_seqs, stride=out_seqs))
    tail = gbuf[tm:tm + hb, :]
    hout_ref[...] = tail
    gbuf[0:hb, :] = tail


def _ffn_call(x, hist, g, wg, wu, cwt, cbias, wd, *, stride, layer, out_seqs=None):
    rows = x.shape[0]
    tm = _row_tile(stride)
    hb = (FFN_CONV_WIDTH - 1) * stride
    kern = functools.partial(_ffn_kernel, tm=tm, hb=hb, stride=stride, cw=256, out_seqs=out_seqs)
    scratch = [pltpu.VMEM((hb + tm, D_FF), F32), pltpu.VMEM((tm, D_FF), BF16)]
    if out_seqs is None:
        out_spec = pl.BlockSpec((tm, D_MODEL), lambda i: (i, 0))
        out_shape = jax.ShapeDtypeStruct((rows, D_MODEL), F32)
    else:
        steps = tm // out_seqs
        out_spec = pl.BlockSpec((out_seqs, steps, D_MODEL), lambda i: (0, i, 0))
        out_shape = jax.ShapeDtypeStruct((out_seqs, rows // out_seqs, D_MODEL), F32)
        scratch.append(pltpu.VMEM((D_MODEL // LANES, tm, LANES), F32))
    return pl.pallas_call(
        kern,
        grid=(rows // tm,),
        in_specs=[
            pl.BlockSpec((tm, D_MODEL), lambda i: (i, 0)),
            _hist_spec(hb, D_FF),
            _resident(), _layer_weight(wg, layer), _layer_weight(wu, layer), _resident(), _resident(),
            _layer_weight(wd, layer),
        ],
        out_specs=[out_spec, _hist_spec(hb, D_FF)],
        out_shape=[out_shape, jax.ShapeDtypeStruct((hb, D_FF), F32)],
        scratch_shapes=scratch,
        compiler_params=_params(1),
        name="conv_ffn",
    )(x, hist, g, wg, wu, cwt, cbias, wd)


CONV_ROWS = 16


def _even_pre_kernel(x_ref, hist_ref, ra_ref, rb_ref, rc_ref, g_ref, win_ref, cw_ref, cb_ref,
                     lng_ref, lnb_ref, qg_ref, kg_ref, pmean_ref,
                     c_ref, q_ref, k_ref, v_ref, hout_ref, *rest, tm, hb, stride, in_seqs):
    if in_seqs is None:
        ubuf, cbuf = rest
        x = x_ref[...]
    else:
        xtm_ref, ubuf, cbuf, xslab = rest
        for b in range(in_seqs):
            for j in range(D_MODEL // LANES):
                xslab[j, pl.ds(b, tm // in_seqs, stride=in_seqs), :] = x_ref[b, :, j * LANES:(j + 1) * LANES]
        x = _slab_cat(xslab)
        xtm_ref[...] = x

    @pl.when(pl.program_id(0) == 0)
    def _():
        ubuf[0:hb, :] = hist_ref[...]

    h = _rms(x, g_ref[...]).astype(BF16)
    proj = _dot(h, _w(win_ref))
    u = proj[:, 0:CONV_A_DIM] * _sigmoid(proj[:, CONV_A_DIM:2 * CONV_A_DIM])
    ubuf[hb:hb + tm, :] = u

    for r0 in range(0, tm, CONV_ROWS):
        acc = jnp.zeros((CONV_ROWS, CONV_A_DIM), F32)
        for j in range(CONV_A_WIDTH):
            off = r0 + j * stride
            acc = acc + cw_ref[j:j + 1, :] * ubuf[off:off + CONV_ROWS, :]
        cbuf[r0:r0 + CONV_ROWS, :] = acc + cb_ref[...]

    conv = cbuf[...]
    mu = jnp.mean(conv, axis=-1, keepdims=True)
    cen = conv - mu
    var = jnp.mean(cen * cen, axis=-1, keepdims=True)
    ln = cen * lax.rsqrt(var + EPS) * lng_ref[...] + lnb_ref[...]
    c_ref[...] = _silu(ln).astype(BF16)

    tail = ubuf[tm:tm + hb, :]
    hout_ref[...] = tail
    ubuf[0:hb, :] = tail

    ra, rb, rc = ra_ref[...], rb_ref[...], rc_ref[...]
    pmean = pmean_ref[...]
    q0 = 2 * CONV_A_DIM
    for s in range(Q_SLABS):
        qs = proj[:, q0 + s * LANES:q0 + (s + 1) * LANES]
        ms = _dot_split(qs * qs, pmean)
        qn = qs * lax.rsqrt(ms + EPS) * qg_ref[...]
        q_ref[s] = _rope(qn, ra, rb, rc)
    ks = proj[:, q0 + QD:q0 + QD + KD]
    ms = _dot_split(ks * ks, pmean)
    kn = ks * lax.rsqrt(ms + EPS) * kg_ref[...]
    k_ref[...] = _rope(kn, ra, rb, rc)
    v_ref[...] = proj[:, q0 + QD + KD:q0 + QD + 2 * KD]


def _even_pre_call(x, hist, ra, rb, rc, g, win, cwt, cbias, lng, lnb, qg, kg, pmean, *, stride, layer,
                   in_seqs=None):
    rows = x.shape[0] if in_seqs is None else x.shape[0] * x.shape[1]
    tm = _row_tile(stride)
    hb = (CONV_A_WIDTH - 1) * stride
    kern = functools.partial(_even_pre_kernel, tm=tm, hb=hb, stride=stride, in_seqs=in_seqs)
    row_spec = lambda w: pl.BlockSpec((tm, w), lambda i: (i, 0))
    x_spec = row_spec(D_MODEL)
    extra_specs, extra_shapes = [], []
    scratch = [pltpu.VMEM((hb + tm, CONV_A_DIM), F32), pltpu.VMEM((tm, CONV_A_DIM), F32)]
    if in_seqs is not None:
        x_spec = pl.BlockSpec((in_seqs, tm // in_seqs, D_MODEL), lambda i: (0, i, 0))
        extra_specs, extra_shapes = [row_spec(D_MODEL)], [jax.ShapeDtypeStruct((rows, D_MODEL), F32)]
        scratch.append(pltpu.VMEM((D_MODEL // LANES, tm, LANES), F32))
    return pl.pallas_call(
        kern,
        grid=(rows // tm,),
        in_specs=[
            x_spec,
            _hist_spec(hb, CONV_A_DIM),
            row_spec(LANES), row_spec(LANES), row_spec(LANES),
            _resident(), _layer_weight(win, layer),
        ] + [_resident()] * 7,
        out_specs=[
            row_spec(CONV_A_DIM),
            pl.BlockSpec((Q_SLABS, tm, LANES), lambda i: (0, i, 0)),
            row_spec(KD), row_spec(KD),
            _hist_spec(hb, CONV_A_DIM),
        ] + extra_specs,
        out_shape=[
            jax.ShapeDtypeStruct((rows, CONV_A_DIM), BF16),
            jax.ShapeDtypeStruct((Q_SLABS, rows, LANES), F32),
            jax.ShapeDtypeStruct((rows, KD), F32),
            jax.ShapeDtypeStruct((rows, KD), F32),
            jax.ShapeDtypeStruct((hb, CONV_A_DIM), F32),
        ] + extra_shapes,
        scratch_shapes=scratch,
        compiler_params=_params(1),
        name="even_pre",
    )(x, hist, ra, rb, rc, g, win, cwt, cbias, lng, lnb, qg, kg, pmean)


def _attn_out_kernel(sink_ref, q_ref, k_ref, v_ref, c_ref, x_ref, wout_ref,
                     out_ref, kprev, vprev, obuf, *, nseq):
    step = pl.program_id(0)

    @pl.when(step == 0)
    def _():
        kprev[...] = jnp.zeros_like(kprev)
        vprev[...] = jnp.zeros_like(vprev)

    row = lax.broadcasted_iota(jnp.int32, (WINDOW, 2 * WINDOW), 0)
    col = lax.broadcasted_iota(jnp.int32, (WINDOW, 2 * WINDOW), 1)
    valid = (col - row >= 1) & (col - row <= WINDOW) & ((col >= WINDOW) | (step > 0))

    def per_seq(b, carry):
        rows = pl.ds(b, WINDOW, stride=nseq)
        kb = k_ref[rows, :].astype(BF16)
        vb = v_ref[rows, :].astype(BF16)
        kk = jnp.concatenate([kprev[b], kb], axis=0)
        vv = jnp.concatenate([vprev[b], vb], axis=0)
        for j in range(Q_SLABS):
            qj = q_ref[j, rows, :].astype(BF16)
            outs = []
            for h2 in range(LANES // HEAD_DIM):
                hd = j * (LANES // HEAD_DIM) + h2
                ls = slice((hd // Q_PER_KV) * HEAD_DIM, (hd // Q_PER_KV + 1) * HEAD_DIM)
                s = _dot_nt(qj[:, h2 * HEAD_DIM:(h2 + 1) * HEAD_DIM], kk[:, ls]) * (HEAD_DIM ** -0.5)
                s = jnp.where(valid, s, -jnp.inf)
                p = _softmax_with_sink(s, sink_ref[hd]).astype(BF16)
                outs.append(_dot(p, vv[:, ls]))
            obuf[j, rows, :] = jnp.concatenate(outs, axis=1)
        kprev[b] = kb
        vprev[b] = vb
        return carry

    lax.fori_loop(0, nseq, per_seq, 0)
    o = _slab_cat(obuf).astype(BF16)
    mixed = (_dot(c_ref[...], _w(wout_ref, rows=slice(0, CONV_A_DIM)))
             + _dot(o, _w(wout_ref, rows=slice(CONV_A_DIM, None))))
    out_ref[...] = x_ref[...] + mixed


def _attn_out_call(sinks, q, k, v, c, x, wout, *, nseq, layer):
    rows = x.shape[0]
    tq = WINDOW * nseq
    kern = functools.partial(_attn_out_kernel, nseq=nseq)
    row_spec = lambda w: pl.BlockSpec((tq, w), lambda i: (i, 0))
    return pl.pallas_call(
        kern,
        grid=(rows // tq,),
        in_specs=[
            pl.BlockSpec(memory_space=pltpu.SMEM),
            pl.BlockSpec((Q_SLABS, tq, LANES), lambda i: (0, i, 0)),
            row_spec(KD), row_spec(KD), row_spec(CONV_A_DIM), row_spec(D_MODEL), _layer_weight(wout, layer),
        ],
        out_specs=row_spec(D_MODEL),
        out_shape=jax.ShapeDtypeStruct((rows, D_MODEL), F32),
        scratch_shapes=[pltpu.VMEM((nseq, WINDOW, KD), BF16), pltpu.VMEM((nseq, WINDOW, KD), BF16),
                        pltpu.VMEM((Q_SLABS, tq, LANES), F32)],
        compiler_params=_params(1),
        name="attn_out",
    )(sinks, q, k, v, c, x, wout)


def _sample_attn_kernel(sink_ref, q_ref, kn_ref, vn_ref, kh_ref, vh_ref, o_ref, *, t):
    bb = SAMPLE_SEQ_BLOCK
    rows = Q_PER_KV * t
    pitch = 2 * WINDOW
    pad = jnp.zeros((pitch - WINDOW - t, KD), F32)
    kall = jnp.concatenate([p for b in range(bb) for p in (kh_ref[b], kn_ref[b], pad)], axis=0).astype(BF16)
    vall = jnp.concatenate([p for b in range(bb) for p in (vh_ref[b], vn_ref[b], pad)], axis=0).astype(BF16)
    r = lax.broadcasted_iota(jnp.int32, (bb * rows, pitch), 0)
    j = lax.broadcasted_iota(jnp.int32, (bb * rows, pitch), 1)
    tq = r % t
    valid = (j > tq) & (j <= tq + WINDOW)
    grp = (lax.broadcasted_iota(jnp.int32, (bb * rows, 1), 0) % rows) // t
    same_seq = (lax.broadcasted_iota(jnp.int32, (bb * rows, bb * pitch), 0) // rows
                == lax.broadcasted_iota(jnp.int32, (bb * rows, bb * pitch), 1) // pitch)
    for kv in range(N_KV_HEADS):
        sink = jnp.zeros((bb * rows, 1), F32)
        for g in range(Q_PER_KV):
            sink = jnp.where(grp == g, sink_ref[kv * Q_PER_KV + g], sink)
        ls = slice(kv * HEAD_DIM, (kv + 1) * HEAD_DIM)
        q = q_ref[:, kv].reshape(bb * rows, HEAD_DIM).astype(BF16)
        s_all = _dot_nt(q, kall[:, ls]) * (HEAD_DIM ** -0.5)
        s = jnp.concatenate([s_all[b * rows:(b + 1) * rows, b * pitch:(b + 1) * pitch] for b in range(bb)], axis=0)
        p = _softmax_with_sink(jnp.where(valid, s, -jnp.inf), sink)
        p_wide = jnp.where(same_seq, jnp.concatenate([p] * bb, axis=1), 0.0).astype(BF16)
        o_ref[:, kv] = _dot(p_wide, vall[:, ls]).reshape(bb, rows, HEAD_DIM)


def _sample_attn_call(sinks, q, kn, vn, kh, vh, layer):
    nb, _, rows, _ = q.shape
    t = kn.shape[1]
    bb = SAMPLE_SEQ_BLOCK
    kern = functools.partial(_sample_attn_kernel, t=t)
    hist_spec = pl.BlockSpec((None, bb, WINDOW, KD), lambda i: (layer, i, 0, 0))
    return pl.pallas_call(
        kern,
        grid=(nb // bb,),
        in_specs=[
            pl.BlockSpec(memory_space=pltpu.SMEM),
            pl.BlockSpec((bb, N_KV_HEADS, rows, HEAD_DIM), lambda i: (i, 0, 0, 0)),
            pl.BlockSpec((bb, t, KD), lambda i: (i, 0, 0)),
            pl.BlockSpec((bb, t, KD), lambda i: (i, 0, 0)),
            hist_spec, hist_spec,
        ],
        out_specs=pl.BlockSpec((bb, N_KV_HEADS, rows, HEAD_DIM), lambda i: (i, 0, 0, 0)),
        out_shape=jax.ShapeDtypeStruct((nb, N_KV_HEADS, rows, HEAD_DIM), F32),
        compiler_params=_params(1),
        name="sample_attn",
    )(sinks, q, kn, vn, kh, vh)


def _even_out_kernel(c_ref, o_ref, x_ref, wout_ref, out_ref):
    mixed = (_dot(c_ref[...], _w(wout_ref, rows=slice(0, CONV_A_DIM)))
             + _dot(o_ref[...].astype(BF16), _w(wout_ref, rows=slice(CONV_A_DIM, None))))
    out_ref[...] = x_ref[...] + mixed


def _even_out_call(c, o, x, wout, *, layer):
    rows = x.shape[0]
    tm = ROW_TILE
    row_spec = lambda w: pl.BlockSpec((tm, w), lambda i: (i, 0))
    return pl.pallas_call(
        _even_out_kernel,
        grid=(rows // tm,),
        in_specs=[row_spec(CONV_A_DIM), row_spec(QD), row_spec(D_MODEL), _layer_weight(wout, layer)],
        out_specs=row_spec(D_MODEL),
        out_shape=jax.ShapeDtypeStruct((rows, D_MODEL), F32),
        compiler_params=_params(1),
        name="even_out",
    )(c, o, x, wout)


def _odd_pre_kernel(x_ref, hist_ref, g_ref, win_ref, cw_ref, cb_ref, dtb_ref,
                    z_ref, xs_ref, b_ref, c_ref, dt_ref, hout_ref, xbuf, *, tm, hb, stride, cw):
    @pl.when(pl.program_id(0) == 0)
    def _():
        xbuf[0:hb, :] = hist_ref[...]

    h = _rms(x_ref[...], g_ref[...]).astype(BF16)
    for j in range(D_INNER // cw):
        cs = slice(j * cw, (j + 1) * cw)
        z_ref[:, cs] = _dot(h, _w(win_ref, cols=cs))
    spb = cw // LANES
    for j in range(SSM_CONV_DIM // cw):
        cs = slice(j * cw, (j + 1) * cw)
        pre = _dot(h, _w(win_ref, cols=slice(D_INNER + j * cw, D_INNER + (j + 1) * cw)))
        xbuf[hb:hb + tm, cs] = pre
        conv = cw_ref[SSM_CONV_WIDTH - 1:SSM_CONV_WIDTH, cs] * pre + cb_ref[:, cs]
        for tap in range(SSM_CONV_WIDTH - 1):
            off = hb - (SSM_CONV_WIDTH - 1 - tap) * stride
            conv = conv + cw_ref[tap:tap + 1, cs] * xbuf[off:off + tm, cs]
        act = _silu(conv)
        for q in range(spb):
            slab = j * spb + q
            piece = act[:, q * LANES:(q + 1) * LANES]
            if slab < X_SLABS:
                xs_ref[slab] = piece
            elif slab < X_SLABS + G_SLABS:
                b_ref[slab - X_SLABS] = piece
            else:
                c_ref[slab - X_SLABS - G_SLABS] = piece
    dt_cols = slice(D_INNER + SSM_CONV_DIM, D_INNER + SSM_CONV_DIM + SSM_HEADS)
    dt_ref[...] = _softplus(_dot(h, _w(win_ref, cols=dt_cols)) + dtb_ref[...])
    tail = xbuf[tm:tm + hb, :]
    hout_ref[...] = tail
    xbuf[0:hb, :] = tail


def _odd_pre_call(x, hist, g, win, cwt, cbias, dtb, *, stride, layer):
    rows = x.shape[0]
    tm = _row_tile(stride)
    hb = (SSM_CONV_WIDTH - 1) * stride
    kern = functools.partial(_odd_pre_kernel, tm=tm, hb=hb, stride=stride, cw=512)
    row_spec = lambda w: pl.BlockSpec((tm, w), lambda i: (i, 0))
    slab_spec = lambda n: pl.BlockSpec((n, tm, LANES), lambda i: (0, i, 0))
    return pl.pallas_call(
        kern,
        grid=(rows // tm,),
        in_specs=[
            row_spec(D_MODEL),
            _hist_spec(hb, SSM_CONV_DIM),
            _resident(), _layer_weight(win, layer), _resident(), _resident(), _resident(),
        ],
        out_specs=[
            row_spec(D_INNER), slab_spec(X_SLABS), slab_spec(G_SLABS), slab_spec(G_SLABS), row_spec(SSM_HEADS),
            _hist_spec(hb, SSM_CONV_DIM),
        ],
        out_shape=[
            jax.ShapeDtypeStruct((rows, D_INNER), F32),
            jax.ShapeDtypeStruct((X_SLABS, rows, LANES), F32),
            jax.ShapeDtypeStruct((G_SLABS, rows, LANES), F32),
            jax.ShapeDtypeStruct((G_SLABS, rows, LANES), F32),
            jax.ShapeDtypeStruct((rows, SSM_HEADS), F32),
            jax.ShapeDtypeStruct((hb, SSM_CONV_DIM), F32),
        ],
        scratch_shapes=[pltpu.VMEM((hb + tm, SSM_CONV_DIM), F32)],
        compiler_params=_params(1),
        name="odd_pre",
    )(x, hist, g, win, cwt, cbias, dtb)


def _ssd_kernel(*refs, nseq, has_prev):
    (xs_ref, b_ref, c_ref, dt_ref, alog_ref, alogt_ref, tril_ref, triu_ref, expand_ref, dskip_ref) = refs[:10]
    y_ref, ht = refs[10 + has_prev], refs[11 + has_prev]
    dtbuf = refs[12 + has_prev]
    L = SSD_CHUNK
    N = SSM_STATE
    step = pl.program_id(0)

    @pl.when(step == 0)
    def _():
        ht[...] = jnp.zeros_like(ht)
        dtbuf[...] = jnp.zeros_like(dtbuf)

    dtbuf[:, 0:SSM_HEADS] = dt_ref[...]
    a_row = -jnp.exp(alog_ref[...])
    a_col = -jnp.exp(alogt_ref[...])
    tril, triu, expand = tril_ref[...], triu_ref[...], expand_ref[...]
    row = lax.broadcasted_iota(jnp.int32, (L, L), 0)
    col = lax.broadcasted_iota(jnp.int32, (L, L), 1)
    causal = row >= col
    lane = lax.broadcasted_iota(jnp.int32, (L, LANES), 1)
    low_half = lane < SSM_HEAD_DIM

    def per_seq(b, carry):
        rows = pl.ds(b, L, stride=nseq)
        dt_wide = dtbuf[rows, :]
        dt = dt_wide[:, 0:SSM_HEADS]
        dtt = dt_wide.T[0:SSM_HEADS, :]
        acum = _dot_f32(tril, dt * a_row)
        acumt = _dot_f32(dtt * a_col, triu)
        last = acum[L - 1:L, :]
        ea = _dot_split(jnp.exp(acum), expand)
        de = _dot_split(jnp.exp(last - acum) * dt, expand)
        cd = _dot_split(jnp.broadcast_to(jnp.exp(last), (SUBLANES, SSM_HEADS)), expand)[0:1, :]

        cbs, y_off, b_t = [], [], []
        for g in range(SSM_GROUPS):
            cg = c_ref[g, rows, :].astype(BF16)
            bg = b_ref[g, rows, :]
            cbs.append(_dot_nt(cg, bg.astype(BF16)))
            y_off.append(_dot(cg, ht[b, g * N:(g + 1) * N, :].astype(BF16)))
            b_t.append(bg.T.astype(BF16))

        xw = []
        for slab in range(X_SLABS):
            g = slab // SLABS_PER_GROUP
            ws = []
            for hd in (2 * slab, 2 * slab + 1):
                diff = jnp.where(causal, acum[:, hd:hd + 1] - acumt[hd:hd + 1, :], -jnp.inf)
                ws.append(cbs[g] * jnp.exp(diff) * dtt[hd:hd + 1, :])
            wcat = jnp.concatenate(ws, axis=1).astype(BF16)
            xpair = xs_ref[slab, rows, :]
            xbd = jnp.concatenate([jnp.where(low_half, xpair, 0.0), jnp.where(low_half, 0.0, xpair)], axis=0)
            ls = slice(slab * LANES, (slab + 1) * LANES)
            gl = slice((slab % SLABS_PER_GROUP) * LANES, (slab % SLABS_PER_GROUP + 1) * LANES)
            y_ref[slab, rows, :] = (_dot(wcat, xbd.astype(BF16)) + ea[:, ls] * y_off[g][:, gl]
                                    + dskip_ref[:, ls] * xpair)
            xw.append((xpair * de[:, ls]).astype(BF16))

        for g in range(SSM_GROUPS):
            xw_g = jnp.concatenate(xw[g * SLABS_PER_GROUP:(g + 1) * SLABS_PER_GROUP], axis=1)
            gs = slice(g * N, (g + 1) * N)
            ht[b, gs, :] = cd[:, g * GROUP_WIDTH:(g + 1) * GROUP_WIDTH] * ht[b, gs, :] + _dot(b_t[g], xw_g)
        return carry

    lax.fori_loop(0, nseq, per_seq, 0)


def _ssd_call(xs, bm, cm, dt, alog, alogt, tril, triu, expand, dskip, prev_state, *, nseq, layer, n_layers):
    rows = dt.shape[0]
    tq = SSD_CHUNK * nseq
    has_prev = prev_state is not None
    kern = functools.partial(_ssd_kernel, nseq=nseq, has_prev=int(has_prev))
    slab_spec = lambda n: pl.BlockSpec((n, tq, LANES), lambda i: (0, i, 0))
    in_specs = [
        slab_spec(X_SLABS), slab_spec(G_SLABS), slab_spec(G_SLABS),
        pl.BlockSpec((tq, SSM_HEADS), lambda i: (i, 0)),
    ] + [_resident()] * 6
    args = [xs, bm, cm, dt, alog, alogt, tril, triu, expand, dskip]
    aliases = {}
    if has_prev:
        in_specs.append(pl.BlockSpec(memory_space=pl.ANY))
        args.append(prev_state)
        aliases = {len(args) - 1: 1}
    return pl.pallas_call(
        kern,
        grid=(rows // tq,),
        in_specs=in_specs,
        out_specs=[
            slab_spec(X_SLABS),
            pl.BlockSpec((None, nseq, GN, GROUP_WIDTH), lambda i: (layer, 0, 0, 0), pipeline_mode=pl.Buffered(1)),
        ],
        out_shape=[
            jax.ShapeDtypeStruct((X_SLABS, rows, LANES), F32),
            jax.ShapeDtypeStruct((n_layers, nseq, GN, GROUP_WIDTH), F32),
        ],
        scratch_shapes=[pltpu.VMEM((tq, LANES), F32)],
        input_output_aliases=aliases,
        compiler_params=_params(1),
        name="ssd_chunk",
    )(*args)


def _sample_ssd_intra_kernel(xs_ref, b_ref, c_ref, dt_ref, alog_ref, expand_ref, gexp_ref, dskip_ref,
                             ydiag_ref, xw_ref, ea_ref, cd_ref, *, t, nb):
    a_row = -jnp.exp(alog_ref[...])
    expand = expand_ref[...]
    gexp = gexp_ref[...]
    blk = lambda l: slice(l * nb, (l + 1) * nb)
    dts = [dt_ref[blk(l), :] for l in range(t)]
    acum = []
    run = None
    for l in range(t):
        step = dts[l] * a_row
        run = step if run is None else run + step
        acum.append(run)
    last = acum[t - 1]
    cd_ref[...] = _dot_split(jnp.exp(last), expand)
    for l in range(t):
        ea_ref[blk(l), :] = _dot_split(jnp.exp(acum[l]), expand)
        xs_l = _slab_cat(xs_ref, blk(l))
        xw_ref[blk(l), :] = xs_l * _dot_split(jnp.exp(last - acum[l]) * dts[l], expand)
        yd = dskip_ref[...] * xs_l
        c_l = _slab_cat(c_ref, blk(l))
        for s in range(l + 1):
            cbx = _dot((c_l * _slab_cat(b_ref, blk(s))).astype(BF16), gexp)
            w = _dot_split(jnp.exp(acum[l] - acum[s]) * dts[s], expand)
            yd = yd + cbx * w * _slab_cat(xs_ref, blk(s))
        ydiag_ref[blk(l), :] = yd


def _sample_ssd_intra_call(xs, bm, cm, dt, alog, expand, gexp, dskip, *, t, nb):
    rows = dt.shape[0]
    kern = functools.partial(_sample_ssd_intra_kernel, t=t, nb=nb)
    return pl.pallas_call(
        kern,
        in_specs=[_resident()] * 8,
        out_specs=[_resident()] * 4,
        out_shape=[
            jax.ShapeDtypeStruct((rows, D_INNER), F32),
            jax.ShapeDtypeStruct((rows, D_INNER), F32),
            jax.ShapeDtypeStruct((rows, D_INNER), F32),
            jax.ShapeDtypeStruct((nb, D_INNER), F32),
        ],
        compiler_params=pltpu.CompilerParams(vmem_limit_bytes=V7X_VMEM_LIMIT_BYTES),
        name="sample_ssd_intra",
    )(xs, bm, cm, dt, alog, expand, gexp, dskip)


def _sample_ssd_state_kernel(*refs, has_prev):
    h0_ref, c_ref, b_ref, xw_ref, ydiag_ref, ea_ref, cd_ref = refs[:7]
    y_ref, hnew_ref = refs[7 + has_prev], refs[8 + has_prev]
    for j in range(SAMPLE_SEQ_BLOCK):
        ht = h0_ref[j].T
        xw = xw_ref[:, j, :].astype(BF16)
        y_parts, new_parts = [], []
        for g in range(SSM_GROUPS):
            ws_ = slice(g * GROUP_WIDTH, (g + 1) * GROUP_WIDTH)
            y_parts.append(_dot(c_ref[g, :, j, :].astype(BF16), ht[:, ws_].astype(BF16)))
            new_parts.append(_dot_tn(b_ref[g, :, j, :].astype(BF16), xw[:, ws_]))
        y_ref[:, j, :] = ydiag_ref[:, j, :] + ea_ref[:, j, :] * jnp.concatenate(y_parts, axis=1)
        hnew = cd_ref[j:j + 1, :] * ht + jnp.concatenate(new_parts, axis=1)
        hnew_ref[j] = hnew.T


def _sample_ssd_state_call(h0_all, cm, bm, xw, ydiag, ea, cd, prev_state, *, layer):
    n_layers, nb = h0_all.shape[:2]
    t = xw.shape[0]
    bb = SAMPLE_SEQ_BLOCK
    has_prev = prev_state is not None
    kern = functools.partial(_sample_ssd_state_kernel, has_prev=int(has_prev))
    tspec = lambda w: pl.BlockSpec((t, bb, w), lambda i: (0, i, 0))
    gspec = pl.BlockSpec((G_SLABS, t, bb, LANES), lambda i: (0, 0, i, 0))
    state_spec = pl.BlockSpec((None, bb, D_INNER, SSM_STATE), lambda i: (layer, i, 0, 0))
    in_specs = [state_spec, gspec, gspec, tspec(D_INNER), tspec(D_INNER), tspec(D_INNER),
                pl.BlockSpec((bb, D_INNER), lambda i: (i, 0))]
    args = [h0_all, cm, bm, xw, ydiag, ea, cd]
    aliases = {}
    if has_prev:
        in_specs.append(pl.BlockSpec(memory_space=pl.ANY))
        args.append(prev_state)
        aliases = {len(args) - 1: 1}
    return pl.pallas_call(
        kern,
        grid=(nb // bb,),
        in_specs=in_specs,
        out_specs=[tspec(D_INNER), state_spec],
        out_shape=[
            jax.ShapeDtypeStruct((t, nb, D_INNER), F32),
            jax.ShapeDtypeStruct((n_layers, nb, D_INNER, SSM_STATE), F32),
        ],
        input_output_aliases=aliases,
        compiler_params=_params(1),
        name="sample_ssd_state",
    )(*args)


def _odd_out_kernel(y_ref, z_ref, x_ref, gn_ref, wout_ref, out_ref, *, y_slabs):
    y = _slab_cat(y_ref) if y_slabs else y_ref[...]
    v = y * _silu(z_ref[...])
    parts = []
    for g in range(SSM_GROUPS):
        vg = v[:, g * GROUP_WIDTH:(g + 1) * GROUP_WIDTH]
        ms = jnp.mean(vg * vg, axis=-1, keepdims=True)
        parts.append(vg * lax.rsqrt(ms + EPS))
    vn = (jnp.concatenate(parts, axis=1) * gn_ref[...]).astype(BF16)
    out_ref[...] = x_ref[...] + _dot(vn, _w(wout_ref))


def _odd_out_call(y, z, x, gn, wout, *, layer):
    rows = x.shape[0]
    tm = ROW_TILE
    y_slabs = y.ndim == 3
    row_spec = lambda w: pl.BlockSpec((tm, w), lambda i: (i, 0))
    slab_spec = pl.BlockSpec((X_SLABS, tm, LANES), lambda i: (0, i, 0))
    return pl.pallas_call(
        functools.partial(_odd_out_kernel, y_slabs=y_slabs),
        grid=(rows // tm,),
        in_specs=[slab_spec if y_slabs else row_spec(D_INNER), row_spec(D_INNER), row_spec(D_MODEL),
                  _resident(), _layer_weight(wout, layer)],
        out_specs=row_spec(D_MODEL),
        out_shape=jax.ShapeDtypeStruct((rows, D_MODEL), F32),
        compiler_params=_params(1),
        name="odd_out",
    )(y, z, x, gn, wout)


def _rope_tables(pos):
    half = ROPE_DIM // 2
    inv_freq = ROPE_THETA ** (-jnp.arange(0, ROPE_DIM, 2, dtype=F32) / ROPE_DIM)
    ang = pos.astype(F32)[:, None] * inv_freq[None, :]
    cos, sin = jnp.cos(ang), jnp.sin(ang)
    n = pos.shape[0]
    rest = HEAD_DIM - ROPE_DIM
    ra = jnp.concatenate([cos, cos, jnp.ones((n, rest), F32)], axis=1)
    rb = jnp.concatenate([jnp.zeros((n, half), F32), sin, jnp.zeros((n, rest), F32)], axis=1)
    rc = jnp.concatenate([-sin, jnp.zeros((n, half + rest), F32)], axis=1)
    rep = LANES // HEAD_DIM
    return tuple(jnp.tile(m, (1, rep)) for m in (ra, rb, rc))


def _head_mean_matrix():
    idx = jnp.arange(LANES) // HEAD_DIM
    return ((idx[:, None] == idx[None, :]).astype(F32) / HEAD_DIM).astype(BF16)


def _head_expand_matrix():
    idx = jnp.arange(D_INNER) // SSM_HEAD_DIM
    return (jnp.arange(SSM_HEADS)[:, None] == idx[None, :]).astype(BF16)


def _group_expand_matrix():
    src = jnp.arange(GN) // SSM_STATE
    dst = jnp.arange(D_INNER) // GROUP_WIDTH
    return (src[:, None] == dst[None, :]).astype(BF16)


def _make_tables(tp, bp, ts, nb):
    tril = jnp.tril(jnp.ones((SSD_CHUNK, SSD_CHUNK), F32))
    return {
        "rope_p": _rope_tables(jnp.repeat(jnp.arange(tp, dtype=jnp.int32), bp)),
        "rope_s": _rope_tables(PAST_LEN + jnp.repeat(jnp.arange(ts, dtype=jnp.int32), nb)),
        "pmean": _head_mean_matrix(),
        "expand": _head_expand_matrix(),
        "gexp": _group_expand_matrix(),
        "tril": tril,
        "triu": tril.T,
    }


def _row(v):
    return v.reshape(1, -1)


def _time_major_hist(state):
    return jnp.swapaxes(state, 0, 1).reshape(-1, state.shape[-1])


def _batch_major_hist(rows, nseq):
    return jnp.swapaxes(rows.reshape(-1, nseq, rows.shape[-1]), 0, 1)


def _even_layer(xp, xs, conv_hist_s, k_hist_all, v_hist_all, layer, norm_g, w_in, conv_w, conv_b, ln_g, ln_b,
                qn_g, kn_g, sinks, w_out, tables, *, bp, tp, nb, ts):
    shared = (_row(norm_g), w_in, conv_w, _row(conv_b), _row(ln_g), _row(ln_b),
              _row(jnp.tile(qn_g, LANES // HEAD_DIM)), _row(jnp.tile(kn_g, LANES // HEAD_DIM)), tables["pmean"])
    keep = CONV_A_WIDTH - 1
    wout = w_out

    hist_p = jnp.zeros((keep * bp, CONV_A_DIM), F32)
    if xp.ndim == 3:
        c, q, k, v, hout, xp = _even_pre_call(xp, hist_p, *tables["rope_p"], *shared, stride=bp, layer=layer,
                                              in_seqs=bp)
    else:
        c, q, k, v, hout = _even_pre_call(xp, hist_p, *tables["rope_p"], *shared, stride=bp, layer=layer)
    xp_new = _attn_out_call(sinks, q, k, v, c, xp, wout, nseq=bp, layer=layer)
    conv_p = _batch_major_hist(hout, bp)
    last = lambda a: _batch_major_hist(a[-WINDOW * bp:], bp).reshape(bp, WINDOW, N_KV_HEADS, HEAD_DIM)
    wk_p, wv_p = last(k), last(v)

    c, q, k, v, hout = _even_pre_call(xs, _time_major_hist(conv_hist_s), *tables["rope_s"], *shared, stride=nb,
                                      layer=layer)
    conv_s = _batch_major_hist(hout, nb)
    k_new = _batch_major_hist(k, nb)
    v_new = _batch_major_hist(v, nb)
    per_slab = LANES // HEAD_DIM
    q_b = q.reshape(N_KV_HEADS, Q_SLABS // N_KV_HEADS, ts, nb, per_slab, HEAD_DIM).transpose(3, 0, 1, 4, 2, 5)
    q_b = q_b.reshape(nb, N_KV_HEADS, Q_PER_KV * ts, HEAD_DIM)
    kh_all = k_hist_all.reshape(-1, nb, WINDOW, KD)
    vh_all = v_hist_all.reshape(-1, nb, WINDOW, KD)
    o = _sample_attn_call(sinks, q_b, k_new, v_new, kh_all, vh_all, layer)
    o = o.reshape(nb, N_KV_HEADS, Q_PER_KV, ts, HEAD_DIM).transpose(3, 0, 1, 2, 4).reshape(ts * nb, QD)
    xs_new = _even_out_call(c, o, xs, wout, layer=layer)
    win_s = lambda hist, new: jnp.concatenate([hist[layer], new], axis=1)[:, -WINDOW:].reshape(
        nb, WINDOW, N_KV_HEADS, HEAD_DIM)
    return xp_new, xs_new, conv_p, conv_s, wk_p, win_s(kh_all, k_new), wv_p, win_s(vh_all, v_new)


def _odd_layer(xp, xs, conv_hist_s, ssm_all, layer, state_p, state_s, norm_g, w_in, conv_w, conv_b, dt_bias, a_log,
               d_skip, gn_g, w_out, tables, *, bp, tp, nb, ts):
    dskip = _row(jnp.repeat(d_skip, SSM_HEAD_DIM))
    gn = _row(gn_g)
    alog = _row(a_log)
    pre_shared = (_row(norm_g), w_in, conv_w, _row(conv_b), _row(dt_bias))
    keep = SSM_CONV_WIDTH - 1
    n_layers = ssm_all.shape[0]

    z, xc, bm, cm, dt, hout = _odd_pre_call(xp, jnp.zeros((keep * bp, SSM_CONV_DIM), F32), *pre_shared, stride=bp,
                                            layer=layer)
    y, state_p = _ssd_call(xc, bm, cm, dt, alog, a_log.reshape(-1, 1), tables["tril"], tables["triu"],
                           tables["expand"], dskip, state_p, nseq=bp, layer=layer, n_layers=n_layers)
    xp_new = _odd_out_call(y, z, xp, gn, w_out, layer=layer)
    conv_p = _batch_major_hist(hout, bp)

    z, xc, bm, cm, dt, hout = _odd_pre_call(xs, _time_major_hist(conv_hist_s), *pre_shared, stride=nb, layer=layer)
    conv_s = _batch_major_hist(hout, nb)
    ydiag, xw, ea, cd = _sample_ssd_intra_call(xc, bm, cm, dt, alog, tables["expand"], tables["gexp"], dskip,
                                               t=ts, nb=nb)
    tm3 = lambda a: a.reshape(ts, nb, a.shape[-1])
    slab4 = lambda a: a.reshape(G_SLABS, ts, nb, LANES)
    y, state_s = _sample_ssd_state_call(ssm_all, slab4(cm), slab4(bm), tm3(xw), tm3(ydiag), tm3(ea), cd, state_s,
                                        layer=layer)
    xs_new = _odd_out_call(y.reshape(ts * nb, D_INNER), z, xs, gn, w_out, layer=layer)
    return xp_new, xs_new, conv_p, conv_s, state_p, state_s


def _ffn_layer(xp, xs, hist_s, layer, norm_g, w_gate, w_up, conv_w, conv_b, w_down, *, bp, nb, last=False):
    keep = FFN_CONV_WIDTH - 1
    shared = (_row(norm_g), w_gate, w_up, conv_w, _row(conv_b), w_down)
    xp_new, hout = _ffn_call(xp, jnp.zeros((keep * bp, D_FF), F32), *shared, stride=bp, layer=layer,
                             out_seqs=bp if last else None)
    ffn_p = _batch_major_hist(hout, bp)
    xs_new, hout = _ffn_call(xs, _time_major_hist(hist_s), *shared, stride=nb, layer=layer)
    ffn_s = _batch_major_hist(hout, nb)
    return xp_new, xs_new, ffn_p, ffn_s


def kernel(x_prompt, x_sample, state_conv_a, cache_win_k, cache_win_v, state_conv_c, state_ssm, state_ffn_conv,
           norm_mix_e, w_in_e, conv_a_w, conv_a_b, ln_a_g, ln_a_b, q_norm_g, k_norm_g, sinks, w_out_e,
           norm_mix_o, w_in_o, conv_c_w, conv_c_b, dt_bias, a_log, d_skip, gnorm_c, w_out_o,
           norm_ffn, w_gate, w_up, ffn_conv_w, ffn_conv_b, w_down):
    bp, tp, _ = x_prompt.shape
    nb, ts, _ = x_sample.shape
    depth = norm_ffn.shape[0]
    n_odd = state_ssm.shape[0]
    tables = _make_tables(tp, bp, ts, nb)
    dims = dict(bp=bp, tp=tp, nb=nb, ts=ts)

    w_gate, w_up, w_down = _cast_bf16_call(w_gate), _cast_bf16_call(w_up), _cast_bf16_call(w_down)
    xp = x_prompt
    xs = jnp.swapaxes(x_sample, 0, 1).reshape(ts * nb, D_MODEL)
    ssm_all = state_ssm.reshape(n_odd, nb, D_INNER, SSM_STATE)
    state_p = state_s = None
    ca_p, ca_s, wk_p, wk_s, wv_p, wv_s = [], [], [], [], [], []
    cc_p, cc_s, ff_p, ff_s = [], [], [], []
    for layer in range(depth):
        i = layer // 2
        if layer % 2 == 0:
            xp, xs, c1, c2, k1, k2, v1, v2 = _even_layer(
                xp, xs, state_conv_a[i], cache_win_k, cache_win_v, i, norm_mix_e[i], w_in_e, conv_a_w[i],
                conv_a_b[i], ln_a_g[i], ln_a_b[i], q_norm_g[i], k_norm_g[i], sinks[i], w_out_e, tables, **dims)
            ca_p.append(c1); ca_s.append(c2); wk_p.append(k1); wk_s.append(k2); wv_p.append(v1); wv_s.append(v2)
        else:
            xp, xs, c1, c2, state_p, state_s = _odd_layer(
                xp, xs, state_conv_c[i], ssm_all, i, state_p, state_s, norm_mix_o[i], w_in_o, conv_c_w[i],
                conv_c_b[i], dt_bias[i], a_log[i], d_skip[i], gnorm_c[i], w_out_o, tables, **dims)
            cc_p.append(c1); cc_s.append(c2)
        xp, xs, f1, f2 = _ffn_layer(xp, xs, state_ffn_conv[layer], layer, norm_ffn[layer], w_gate, w_up,
                                    ffn_conv_w[layer], ffn_conv_b[layer], w_down, bp=bp, nb=nb,
                                    last=layer == depth - 1)
        ff_p.append(f1); ff_s.append(f2)

    y_prompt = xp
    y_sample = jnp.swapaxes(xs.reshape(ts, nb, D_MODEL), 0, 1)
    ssm_p = state_p.reshape(n_odd, bp, SSM_GROUPS, SSM_STATE, HEADS_PER_GROUP, SSM_HEAD_DIM)
    ssm_p = ssm_p.transpose(0, 1, 2, 4, 5, 3).reshape(n_odd, bp, SSM_HEADS, SSM_HEAD_DIM, SSM_STATE)
    ssm_s = state_s.reshape(n_odd, nb, SSM_HEADS, SSM_HEAD_DIM, SSM_STATE)
    return (y_prompt, y_sample,
            jnp.stack(ca_p), jnp.stack(ca_s),
            jnp.stack(wk_p), jnp.stack(wk_s),
            jnp.stack(wv_p), jnp.stack(wv_s),
            jnp.stack(cc_p), jnp.stack(cc_s),
            ssm_p, ssm_s,
            jnp.stack(ff_p), jnp.stack(ff_s))
```

```python
import functools

import jax
import jax.numpy as jnp
from jax import lax
from jax.experimental import pallas as pl
from jax.experimental.pallas import tpu as pltpu

F32 = jnp.float32
BF16 = jnp.bfloat16

D_MODEL = 1024
PAST_LEN = 8192
CONV_A_DIM = 512
CONV_A_WIDTH = 31
HEAD_DIM = 64
N_Q_HEADS = 8
N_KV_HEADS = 2
Q_PER_KV = N_Q_HEADS // N_KV_HEADS
QD = N_Q_HEADS * HEAD_DIM
KD = N_KV_HEADS * HEAD_DIM
WINDOW = 128
ROPE_DIM = 16
ROPE_THETA = 500000.0
D_INNER = 2048
SSM_HEAD_DIM = 64
SSM_HEADS = 32
SSM_STATE = 128
SSM_GROUPS = 4
HEADS_PER_GROUP = SSM_HEADS // SSM_GROUPS
GROUP_WIDTH = D_INNER // SSM_GROUPS
GN = SSM_GROUPS * SSM_STATE
SSM_CONV_WIDTH = 4
SSM_CONV_DIM = D_INNER + 2 * GN
SSD_CHUNK = 128
D_FF = 2816
FFN_CONV_WIDTH = 3
EPS = 1e-6

V7X_VMEM_LIMIT_BYTES = 58 * 1024 * 1024
SUBLANES = 8
LANES = 128
Q_SLABS = QD // LANES
X_SLABS = D_INNER // LANES
G_SLABS = GN // LANES
SLABS_PER_GROUP = GROUP_WIDTH // LANES

ROW_TILE = 512
SEQ_UNROLL = 2
SAMPLE_SEQ_BLOCK = 8


def _params(n_grid_dims):
    return pltpu.CompilerParams(
        dimension_semantics=("arbitrary",) * n_grid_dims,
        vmem_limit_bytes=V7X_VMEM_LIMIT_BYTES,
    )


def _resident():
    return pl.BlockSpec(memory_space=pltpu.VMEM)


def _layer_weight(stacked, layer):
    _, k, n = stacked.shape
    return pl.BlockSpec((None, k, n), lambda i: (layer, 0, 0), pipeline_mode=pl.Buffered(1))


def _w(ref, rows=slice(None), cols=slice(None)):
    return ref[rows, cols].astype(BF16)


def _row_tile(stride):
    return ROW_TILE if stride <= SUBLANES else ROW_TILE // 2


def _hist_spec(hb, width):
    return pl.BlockSpec((hb, width), lambda i: (0, 0), pipeline_mode=pl.Buffered(1))


def _rms(x, g_row):
    ms = jnp.mean(x * x, axis=-1, keepdims=True)
    return (x * lax.rsqrt(ms + EPS)) * g_row


def _sigmoid(x):
    return 1.0 / (1.0 + jnp.exp(-x))


def _silu(x):
    return x * _sigmoid(x)


def _softplus(x):
    return jnp.maximum(x, 0.0) + jnp.log(1.0 + jnp.exp(-jnp.abs(x)))


def _dot(a, b):
    return jnp.dot(a, b, preferred_element_type=F32)


def _dot_nt(a, b):
    return lax.dot_general(a, b, (((1,), (1,)), ((), ())), preferred_element_type=F32)


def _dot_tn(a, b):
    return lax.dot_general(a, b, (((0,), (0,)), ((), ())), preferred_element_type=F32)


def _bf16_terms(a, n):
    terms, rest = [], a
    for _ in range(n):
        t = rest.astype(BF16)
        terms.append(t)
        rest = rest - t.astype(F32)
    return terms


def _dot_split(a, m2_bf16):
    return _dot(jnp.concatenate(_bf16_terms(a, 2), axis=1), m2_bf16)


def _stack2(m):
    return jnp.concatenate([m, m], axis=0)


def _cumsum_rows(tril_bf16, x):
    n = x.shape[1]
    y = _dot(tril_bf16, jnp.concatenate(_bf16_terms(x, 3), axis=1))
    return y[:, 0:n] + y[:, n:2 * n] + y[:, 2 * n:3 * n]


def _cumsum_cols(x, triu_bf16):
    n = x.shape[0]
    y = _dot(jnp.concatenate(_bf16_terms(x, 3), axis=0), triu_bf16)
    return y[0:n] + y[n:2 * n] + y[2 * n:3 * n]


def _rope(x, ra, rb, rc):
    return x * ra + pltpu.roll(x, 8, 1) * rb + pltpu.roll(x, LANES - 8, 1) * rc


def _softmax_with_sink(s, sink):
    m = jnp.maximum(jnp.max(s, axis=-1, keepdims=True), sink)
    p = jnp.exp(s - m)
    den = jnp.sum(p, axis=-1, keepdims=True) + jnp.exp(sink - m)
    return p / den


def _slab_cat(ref, rows=slice(None)):
    return jnp.concatenate([ref[j, rows, :] for j in range(ref.shape[0])], axis=1)


def _cast_kernel(src_ref, dst_ref):
    dst_ref[...] = src_ref[...].astype(dst_ref.dtype)


def _cast_bf16_call(stacked):
    n_layers, k, n = stacked.shape
    kb = k // 4
    spec = pl.BlockSpec((None, kb, n), lambda l, i: (l, i, 0))
    return pl.pallas_call(
        _cast_kernel,
        grid=(n_layers, k // kb),
        in_specs=[spec],
        out_specs=spec,
        out_shape=jax.ShapeDtypeStruct(stacked.shape, BF16),
        compiler_params=_params(2),
        name="cast_bf16",
    )(stacked)


def _ffn_kernel(x_ref, hist_ref, g_ref, wg_ref, wu_ref, cw_ref, cb_ref, wd_ref,
                out_ref, hout_ref, gbuf, act, *scratch, tm, hb, stride, cw, out_seqs):
    @pl.when(pl.program_id(0) == 0)
    def _():
        gbuf[0:hb, :] = hist_ref[...]

    x = x_ref[...]
    h = _rms(x, g_ref[...]).astype(BF16)
    for j in range(D_FF // cw):
        cs = slice(j * cw, (j + 1) * cw)
        gate = _dot(h, _w(wg_ref, cols=cs))
        up = _dot(h, _w(wu_ref, cols=cs))
        gbuf[hb:hb + tm, cs] = gate
        conv = (cw_ref[0:1, cs] * gbuf[hb - 2 * stride:hb - 2 * stride + tm, cs]
                + cw_ref[1:2, cs] * gbuf[hb - stride:hb - stride + tm, cs]
                + cw_ref[2:3, cs] * gate + cb_ref[:, cs])
        act[:, cs] = (_silu(conv) * up).astype(BF16)
    y = x + _dot(act[...], _w(wd_ref))
    if out_seqs is None:
        out_ref[...] = y
    else:
        (yslab,) = scratch
        for j in range(D_MODEL // LANES):
            yslab[j] = y[:, j * LANES:(j + 1) * LANES]
        for b in range(out_seqs):
            out_ref[b] = _slab_cat(yslab, pl.ds(b, tm // out_seqs, stride=out_seqs))
    tail = gbuf[tm:tm + hb, :]
    hout_ref[...] = tail
    gbuf[0:hb, :] = tail


def _ffn_call(x, hist, g, wg, wu, cwt, cbias, wd, *, stride, layer, out_seqs=None):
    rows = x.shape[0]
    tm = _row_tile(stride)
    hb = (FFN_CONV_WIDTH - 1) * stride
    kern = functools.partial(_ffn_kernel, tm=tm, hb=hb, stride=stride, cw=256, out_seqs=out_seqs)
    scratch = [pltpu.VMEM((hb + tm, D_FF), F32), pltpu.VMEM((tm, D_FF), BF16)]
    if out_seqs is None:
        out_spec = pl.BlockSpec((tm, D_MODEL), lambda i: (i, 0))
        out_shape = jax.ShapeDtypeStruct((rows, D_MODEL), F32)
    else:
        steps = tm // out_seqs
        out_spec = pl.BlockSpec((out_seqs, steps, D_MODEL), lambda i: (0, i, 0))
        out_shape = jax.ShapeDtypeStruct((out_seqs, rows // out_seqs, D_MODEL), F32)
        scratch.append(pltpu.VMEM((D_MODEL // LANES, tm, LANES), F32))
    return pl.pallas_call(
        kern,
        grid=(rows // tm,),
        in_specs=[
            pl.BlockSpec((tm, D_MODEL), lambda i: (i, 0)),
            _hist_spec(hb, D_FF),
            _resident(), _layer_weight(wg, layer), _layer_weight(wu, layer), _resident(), _resident(),
            _layer_weight(wd, layer),
        ],
        out_specs=[out_spec, _hist_spec(hb, D_FF)],
        out_shape=[out_shape, jax.ShapeDtypeStruct((hb, D_FF), F32)],
        scratch_shapes=scratch,
        compiler_params=_params(1),
        name="conv_ffn",
    )(x, hist, g, wg, wu, cwt, cbias, wd)


CONV_ROWS = 16


def _even_pre_kernel(x_ref, hist_ref, ra_ref, rb_ref, rc_ref, g_ref, win_ref, cw_ref, cb_ref,
                     lng_ref, lnb_ref, qg_ref, kg_ref, pmean_ref,
                     c_ref, q_ref, k_ref, v_ref, hout_ref, *rest, tm, hb, stride, in_seqs):
    if in_seqs is None:
        ubuf, cbuf = rest
        x = x_ref[...]
    else:
        xtm_ref, ubuf, cbuf, xslab = rest
        for b in range(in_seqs):
            for j in range(D_MODEL // LANES):
                xslab[j, pl.ds(b, tm // in_seqs, stride=in_seqs), :] = x_ref[b, :, j * LANES:(j + 1) * LANES]
        x = _slab_cat(xslab)
        xtm_ref[...] = x

    @pl.when(pl.program_id(0) == 0)
    def _():
        ubuf[0:hb, :] = hist_ref[...]

    h = _rms(x, g_ref[...]).astype(BF16)
    proj = _dot(h, _w(win_ref))
    u = proj[:, 0:CONV_A_DIM] * _sigmoid(proj[:, CONV_A_DIM:2 * CONV_A_DIM])
    ubuf[hb:hb + tm, :] = u

    for r0 in range(0, tm, CONV_ROWS):
        acc = jnp.zeros((CONV_ROWS, CONV_A_DIM), F32)
        for j in range(CONV_A_WIDTH):
            off = r0 + j * stride
            acc = acc + cw_ref[j:j + 1, :] * ubuf[off:off + CONV_ROWS, :]
        cbuf[r0:r0 + CONV_ROWS, :] = acc + cb_ref[...]

    conv = cbuf[...]
    mu = jnp.mean(conv, axis=-1, keepdims=True)
    cen = conv - mu
    var = jnp.mean(cen * cen, axis=-1, keepdims=True)
    ln = cen * lax.rsqrt(var + EPS) * lng_ref[...] + lnb_ref[...]
    c_ref[...] = _silu(ln).astype(BF16)

    tail = ubuf[tm:tm + hb, :]
    hout_ref[...] = tail
    ubuf[0:hb, :] = tail

    ra, rb, rc = ra_ref[...], rb_ref[...], rc_ref[...]
    pmean = pmean_ref[...]
    q0 = 2 * CONV_A_DIM
    for s in range(Q_SLABS):
        qs = proj[:, q0 + s * LANES:q0 + (s + 1) * LANES]
        ms = _dot_split(qs * qs, pmean)
        qn = qs * lax.rsqrt(ms + EPS) * qg_ref[...]
        q_ref[s] = _rope(qn, ra, rb, rc)
    ks = proj[:, q0 + QD:q0 + QD + KD]
    ms = _dot_split(ks * ks, pmean)
    kn = ks * lax.rsqrt(ms + EPS) * kg_ref[...]
    k_ref[...] = _rope(kn, ra, rb, rc)
    v_ref[...] = proj[:, q0 + QD + KD:q0 + QD + 2 * KD]


def _even_pre_call(x, hist, ra, rb, rc, g, win, cwt, cbias, lng, lnb, qg, kg, pmean, *, stride, layer,
                   in_seqs=None):
    rows = x.shape[0] if in_seqs is None else x.shape[0] * x.shape[1]
    tm = _row_tile(stride)
    hb = (CONV_A_WIDTH - 1) * stride
    kern = functools.partial(_even_pre_kernel, tm=tm, hb=hb, stride=stride, in_seqs=in_seqs)
    row_spec = lambda w: pl.BlockSpec((tm, w), lambda i: (i, 0))
    x_spec = row_spec(D_MODEL)
    extra_specs, extra_shapes = [], []
    scratch = [pltpu.VMEM((hb + tm, CONV_A_DIM), F32), pltpu.VMEM((tm, CONV_A_DIM), F32)]
    if in_seqs is not None:
        x_spec = pl.BlockSpec((in_seqs, tm // in_seqs, D_MODEL), lambda i: (0, i, 0))
        extra_specs, extra_shapes = [row_spec(D_MODEL)], [jax.ShapeDtypeStruct((rows, D_MODEL), F32)]
        scratch.append(pltpu.VMEM((D_MODEL // LANES, tm, LANES), F32))
    return pl.pallas_call(
        kern,
        grid=(rows // tm,),
        in_specs=[
            x_spec,
            _hist_spec(hb, CONV_A_DIM),
            row_spec(LANES), row_spec(LANES), row_spec(LANES),
            _resident(), _layer_weight(win, layer),
        ] + [_resident()] * 7,
        out_specs=[
            row_spec(CONV_A_DIM),
            pl.BlockSpec((Q_SLABS, tm, LANES), lambda i: (0, i, 0)),
            row_spec(KD), row_spec(KD),
            _hist_spec(hb, CONV_A_DIM),
        ] + extra_specs,
        out_shape=[
            jax.ShapeDtypeStruct((rows, CONV_A_DIM), BF16),
            jax.ShapeDtypeStruct((Q_SLABS, rows, LANES), F32),
            jax.ShapeDtypeStruct((rows, KD), F32),
            jax.ShapeDtypeStruct((rows, KD), F32),
            jax.ShapeDtypeStruct((hb, CONV_A_DIM), F32),
        ] + extra_shapes,
        scratch_shapes=scratch,
        compiler_params=_params(1),
        name="even_pre",
    )(x, hist, ra, rb, rc, g, win, cwt, cbias, lng, lnb, qg, kg, pmean)


def _attn_out_kernel(sink_ref, q_ref, k_ref, v_ref, c_ref, x_ref, wout_ref,
                     out_ref, kprev, vprev, obuf, *, nseq):
    step = pl.program_id(0)

    @pl.when(step == 0)
    def _():
        kprev[...] = jnp.zeros_like(kprev)
        vprev[...] = jnp.zeros_like(vprev)

    row = lax.broadcasted_iota(jnp.int32, (WINDOW, 2 * WINDOW), 0)
    col = lax.broadcasted_iota(jnp.int32, (WINDOW, 2 * WINDOW), 1)
    valid = (col - row >= 1) & (col - row <= WINDOW) & ((col >= WINDOW) | (step > 0))

    def per_seq(b, carry):
        rows = pl.ds(b, WINDOW, stride=nseq)
        kb = k_ref[rows, :].astype(BF16)
        vb = v_ref[rows, :].astype(BF16)
        kk = jnp.concatenate([kprev[b], kb], axis=0)
        vv = jnp.concatenate([vprev[b], vb], axis=0)
        for j in range(Q_SLABS):
            qj = (q_ref[j, rows, :] * (HEAD_DIM ** -0.5)).astype(BF16)
            outs = []
            for h2 in range(LANES // HEAD_DIM):
                hd = j * (LANES // HEAD_DIM) + h2
                ls = slice((hd // Q_PER_KV) * HEAD_DIM, (hd // Q_PER_KV + 1) * HEAD_DIM)
                s = _dot_nt(qj[:, h2 * HEAD_DIM:(h2 + 1) * HEAD_DIM], kk[:, ls])
                s = jnp.where(valid, s, -jnp.inf)
                p = _softmax_with_sink(s, sink_ref[hd]).astype(BF16)
                outs.append(_dot(p, vv[:, ls]))
            obuf[j, rows, :] = jnp.concatenate(outs, axis=1)
        kprev[b] = kb
        vprev[b] = vb
        return carry

    lax.fori_loop(0, nseq, per_seq, 0, unroll=SEQ_UNROLL)
    o = _slab_cat(obuf).astype(BF16)
    mixed = (_dot(c_ref[...], _w(wout_ref, rows=slice(0, CONV_A_DIM)))
             + _dot(o, _w(wout_ref, rows=slice(CONV_A_DIM, None))))
    out_ref[...] = x_ref[...] + mixed


def _attn_out_call(sinks, q, k, v, c, x, wout, *, nseq, layer):
    rows = x.shape[0]
    tq = WINDOW * nseq
    kern = functools.partial(_attn_out_kernel, nseq=nseq)
    row_spec = lambda w: pl.BlockSpec((tq, w), lambda i: (i, 0))
    return pl.pallas_call(
        kern,
        grid=(rows // tq,),
        in_specs=[
            pl.BlockSpec(memory_space=pltpu.SMEM),
            pl.BlockSpec((Q_SLABS, tq, LANES), lambda i: (0, i, 0)),
            row_spec(KD), row_spec(KD), row_spec(CONV_A_DIM), row_spec(D_MODEL), _layer_weight(wout, layer),
        ],
        out_specs=row_spec(D_MODEL),
        out_shape=jax.ShapeDtypeStruct((rows, D_MODEL), F32),
        scratch_shapes=[pltpu.VMEM((nseq, WINDOW, KD), BF16), pltpu.VMEM((nseq, WINDOW, KD), BF16),
                        pltpu.VMEM((Q_SLABS, tq, LANES), F32)],
        compiler_params=_params(1),
        name="attn_out",
    )(sinks, q, k, v, c, x, wout)


def _sample_attn_kernel(sink_ref, q_ref, kn_ref, vn_ref, kh_ref, vh_ref, o_ref, *, t):
    bb = SAMPLE_SEQ_BLOCK
    rows = Q_PER_KV * t
    pitch = 2 * WINDOW
    pad = jnp.zeros((pitch - WINDOW - t, KD), F32)
    kall = jnp.concatenate([p for b in range(bb) for p in (kh_ref[b], kn_ref[b], pad)], axis=0).astype(BF16)
    vall = jnp.concatenate([p for b in range(bb) for p in (vh_ref[b], vn_ref[b], pad)], axis=0).astype(BF16)
    r = lax.broadcasted_iota(jnp.int32, (bb * rows, pitch), 0)
    j = lax.broadcasted_iota(jnp.int32, (bb * rows, pitch), 1)
    tq = r % t
    valid = (j > tq) & (j <= tq + WINDOW)
    grp = (lax.broadcasted_iota(jnp.int32, (bb * rows, 1), 0) % rows) // t
    same_seq = (lax.broadcasted_iota(jnp.int32, (bb * rows, bb * pitch), 0) // rows
                == lax.broadcasted_iota(jnp.int32, (bb * rows, bb * pitch), 1) // pitch)
    for kv in range(N_KV_HEADS):
        sink = jnp.zeros((bb * rows, 1), F32)
        for g in range(Q_PER_KV):
            sink = jnp.where(grp == g, sink_ref[kv * Q_PER_KV + g], sink)
        ls = slice(kv * HEAD_DIM, (kv + 1) * HEAD_DIM)
        q = q_ref[:, kv].reshape(bb * rows, HEAD_DIM).astype(BF16)
        s_all = _dot_nt(q, kall[:, ls]) * (HEAD_DIM ** -0.5)
        s = jnp.concatenate([s_all[b * rows:(b + 1) * rows, b * pitch:(b + 1) * pitch] for b in range(bb)], axis=0)
        p = _softmax_with_sink(jnp.where(valid, s, -jnp.inf), sink)
        p_wide = jnp.where(same_seq, jnp.concatenate([p] * bb, axis=1), 0.0).astype(BF16)
        o_ref[:, kv] = _dot(p_wide, vall[:, ls]).reshape(bb, rows, HEAD_DIM)


def _sample_attn_call(sinks, q, kn, vn, kh, vh, layer):
    nb, _, rows, _ = q.shape
    t = kn.shape[1]
    bb = SAMPLE_SEQ_BLOCK
    kern = functools.partial(_sample_attn_kernel, t=t)
    hist_spec = pl.BlockSpec((None, bb, WINDOW, KD), lambda i: (layer, i, 0, 0))
    return pl.pallas_call(
        kern,
        grid=(nb // bb,),
        in_specs=[
            pl.BlockSpec(memory_space=pltpu.SMEM),
            pl.BlockSpec((bb, N_KV_HEADS, rows, HEAD_DIM), lambda i: (i, 0, 0, 0)),
            pl.BlockSpec((bb, t, KD), lambda i: (i, 0, 0)),
            pl.BlockSpec((bb, t, KD), lambda i: (i, 0, 0)),
            hist_spec, hist_spec,
        ],
        out_specs=pl.BlockSpec((bb, N_KV_HEADS, rows, HEAD_DIM), lambda i: (i, 0, 0, 0)),
        out_shape=jax.ShapeDtypeStruct((nb, N_KV_HEADS, rows, HEAD_DIM), F32),
        compiler_params=_params(1),
        name="sample_attn",
    )(sinks, q, kn, vn, kh, vh)


def _even_out_kernel(c_ref, o_ref, x_ref, wout_ref, out_ref):
    mixed = (_dot(c_ref[...], _w(wout_ref, rows=slice(0, CONV_A_DIM)))
             + _dot(o_ref[...].astype(BF16), _w(wout_ref, rows=slice(CONV_A_DIM, None))))
    out_ref[...] = x_ref[...] + mixed


def _even_out_call(c, o, x, wout, *, layer):
    rows = x.shape[0]
    tm = ROW_TILE
    row_spec = lambda w: pl.BlockSpec((tm, w), lambda i: (i, 0))
    return pl.pallas_call(
        _even_out_kernel,
        grid=(rows // tm,),
        in_specs=[row_spec(CONV_A_DIM), row_spec(QD), row_spec(D_MODEL), _layer_weight(wout, layer)],
        out_specs=row_spec(D_MODEL),
        out_shape=jax.ShapeDtypeStruct((rows, D_MODEL), F32),
        compiler_params=_params(1),
        name="even_out",
    )(c, o, x, wout)


def _odd_pre_kernel(x_ref, hist_ref, g_ref, win_ref, cw_ref, cb_ref, dtb_ref,
                    z_ref, xs_ref, b_ref, c_ref, dt_ref, hout_ref, xbuf, *, tm, hb, stride, cw):
    @pl.when(pl.program_id(0) == 0)
    def _():
        xbuf[0:hb, :] = hist_ref[...]

    h = _rms(x_ref[...], g_ref[...]).astype(BF16)
    for j in range(D_INNER // cw):
        cs = slice(j * cw, (j + 1) * cw)
        z_ref[:, cs] = _dot(h, _w(win_ref, cols=cs))
    spb = cw // LANES
    for j in range(SSM_CONV_DIM // cw):
        cs = slice(j * cw, (j + 1) * cw)
        pre = _dot(h, _w(win_ref, cols=slice(D_INNER + j * cw, D_INNER + (j + 1) * cw)))
        xbuf[hb:hb + tm, cs] = pre
        conv = cw_ref[SSM_CONV_WIDTH - 1:SSM_CONV_WIDTH, cs] * pre + cb_ref[:, cs]
        for tap in range(SSM_CONV_WIDTH - 1):
            off = hb - (SSM_CONV_WIDTH - 1 - tap) * stride
            conv = conv + cw_ref[tap:tap + 1, cs] * xbuf[off:off + tm, cs]
        act = _silu(conv)
        for q in range(spb):
            slab = j * spb + q
            piece = act[:, q * LANES:(q + 1) * LANES]
            if slab < X_SLABS:
                xs_ref[slab] = piece
            elif slab < X_SLABS + G_SLABS:
                b_ref[slab - X_SLABS] = piece
            else:
                c_ref[slab - X_SLABS - G_SLABS] = piece
    dt_cols = slice(D_INNER + SSM_CONV_DIM, D_INNER + SSM_CONV_DIM + SSM_HEADS)
    dt_ref[...] = _softplus(_dot(h, _w(win_ref, cols=dt_cols)) + dtb_ref[...])
    tail = xbuf[tm:tm + hb, :]
    hout_ref[...] = tail
    xbuf[0:hb, :] = tail


def _odd_pre_call(x, hist, g, win, cwt, cbias, dtb, *, stride, layer):
    rows = x.shape[0]
    tm = _row_tile(stride)
    hb = (SSM_CONV_WIDTH - 1) * stride
    kern = functools.partial(_odd_pre_kernel, tm=tm, hb=hb, stride=stride, cw=512)
    row_spec = lambda w: pl.BlockSpec((tm, w), lambda i: (i, 0))
    slab_spec = lambda n: pl.BlockSpec((n, tm, LANES), lambda i: (0, i, 0))
    return pl.pallas_call(
        kern,
        grid=(rows // tm,),
        in_specs=[
            row_spec(D_MODEL),
            _hist_spec(hb, SSM_CONV_DIM),
            _resident(), _layer_weight(win, layer), _resident(), _resident(), _resident(),
        ],
        out_specs=[
            row_spec(D_INNER), slab_spec(X_SLABS), slab_spec(G_SLABS), slab_spec(G_SLABS), row_spec(SSM_HEADS),
            _hist_spec(hb, SSM_CONV_DIM),
        ],
        out_shape=[
            jax.ShapeDtypeStruct((rows, D_INNER), F32),
            jax.ShapeDtypeStruct((X_SLABS, rows, LANES), F32),
            jax.ShapeDtypeStruct((G_SLABS, rows, LANES), F32),
            jax.ShapeDtypeStruct((G_SLABS, rows, LANES), F32),
            jax.ShapeDtypeStruct((rows, SSM_HEADS), F32),
            jax.ShapeDtypeStruct((hb, SSM_CONV_DIM), F32),
        ],
        scratch_shapes=[pltpu.VMEM((hb + tm, SSM_CONV_DIM), F32)],
        compiler_params=_params(1),
        name="odd_pre",
    )(x, hist, g, win, cwt, cbias, dtb)


def _ssd_kernel(*refs, nseq, has_prev):
    (xs_ref, b_ref, c_ref, dt_ref, alog_ref, alogt_ref, tril_ref, triu_ref, expand_ref, dskip_ref) = refs[:10]
    y_ref, ht = refs[10 + has_prev], refs[11 + has_prev]
    dtbuf = refs[12 + has_prev]
    L = SSD_CHUNK
    N = SSM_STATE
    step = pl.program_id(0)

    @pl.when(step == 0)
    def _():
        ht[...] = jnp.zeros_like(ht)
        dtbuf[...] = jnp.zeros_like(dtbuf)

    dtbuf[:, 0:SSM_HEADS] = dt_ref[...]
    a_row = -jnp.exp(alog_ref[...])
    a_col = -jnp.exp(alogt_ref[...])
    tril, triu, expand = tril_ref[...], triu_ref[...], expand_ref[...]
    row = lax.broadcasted_iota(jnp.int32, (L, L), 0)
    col = lax.broadcasted_iota(jnp.int32, (L, L), 1)
    causal = row >= col
    lane = lax.broadcasted_iota(jnp.int32, (L, LANES), 1)
    low_half = lane < SSM_HEAD_DIM

    def per_seq(b, carry):
        rows = pl.ds(b, L, stride=nseq)
        dt_wide = dtbuf[rows, :]
        dt = dt_wide[:, 0:SSM_HEADS]
        dtt = dt_wide.T[0:SSM_HEADS, :]
        acum = _cumsum_rows(tril, dt * a_row)
        acumt = _cumsum_cols(dtt * a_col, triu)
        last = acum[L - 1:L, :]
        ea = _dot_split(jnp.exp(acum), expand)
        de = _dot_split(jnp.exp(last - acum) * dt, expand)
        cd = _dot_split(jnp.broadcast_to(jnp.exp(last), (SUBLANES, SSM_HEADS)), expand)[0:1, :]

        cbs, y_off, b_t = [], [], []
        for g in range(SSM_GROUPS):
            cg = c_ref[g, rows, :].astype(BF16)
            bg = b_ref[g, rows, :]
            cbs.append(_dot_nt(cg, bg.astype(BF16)))
            y_off.append(_dot(cg, ht[b, g * N:(g + 1) * N, :].astype(BF16)))
            b_t.append(bg.T.astype(BF16))

        xw = []
        for slab in range(X_SLABS):
            g = slab // SLABS_PER_GROUP
            ws = []
            for hd in (2 * slab, 2 * slab + 1):
                diff = jnp.where(causal, acum[:, hd:hd + 1] - acumt[hd:hd + 1, :], -jnp.inf)
                ws.append(cbs[g] * jnp.exp(diff) * dtt[hd:hd + 1, :])
            wcat = jnp.concatenate(ws, axis=1).astype(BF16)
            xpair = xs_ref[slab, rows, :]
            xbd = jnp.concatenate([jnp.where(low_half, xpair, 0.0), jnp.where(low_half, 0.0, xpair)], axis=0)
            ls = slice(slab * LANES, (slab + 1) * LANES)
            gl = slice((slab % SLABS_PER_GROUP) * LANES, (slab % SLABS_PER_GROUP + 1) * LANES)
            y_ref[slab, rows, :] = (_dot(wcat, xbd.astype(BF16)) + ea[:, ls] * y_off[g][:, gl]
                                    + dskip_ref[:, ls] * xpair)
            xw.append((xpair * de[:, ls]).astype(BF16))

        for g in range(SSM_GROUPS):
            xw_g = jnp.concatenate(xw[g * SLABS_PER_GROUP:(g + 1) * SLABS_PER_GROUP], axis=1)
            gs = slice(g * N, (g + 1) * N)
            ht[b, gs, :] = cd[:, g * GROUP_WIDTH:(g + 1) * GROUP_WIDTH] * ht[b, gs, :] + _dot(b_t[g], xw_g)
        return carry

    lax.fori_loop(0, nseq, per_seq, 0)


def _ssd_call(xs, bm, cm, dt, alog, alogt, tril, triu, expand, dskip, prev_state, *, nseq, layer, n_layers):
    rows = dt.shape[0]
    tq = SSD_CHUNK * nseq
    has_prev = prev_state is not None
    kern = functools.partial(_ssd_kernel, nseq=nseq, has_prev=int(has_prev))
    slab_spec = lambda n: pl.BlockSpec((n, tq, LANES), lambda i: (0, i, 0))
    in_specs = [
        slab_spec(X_SLABS), slab_spec(G_SLABS), slab_spec(G_SLABS),
        pl.BlockSpec((tq, SSM_HEADS), lambda i: (i, 0)),
    ] + [_resident()] * 6
    args = [xs, bm, cm, dt, alog, alogt, tril, triu, expand, dskip]
    aliases = {}
    if has_prev:
        in_specs.append(pl.BlockSpec(memory_space=pl.ANY))
        args.append(prev_state)
        aliases = {len(args) - 1: 1}
    return pl.pallas_call(
        kern,
        grid=(rows // tq,),
        in_specs=in_specs,
        out_specs=[
            slab_spec(X_SLABS),
            pl.BlockSpec((None, nseq, GN, GROUP_WIDTH), lambda i: (layer, 0, 0, 0), pipeline_mode=pl.Buffered(1)),
        ],
        out_shape=[
            jax.ShapeDtypeStruct((X_SLABS, rows, LANES), F32),
            jax.ShapeDtypeStruct((n_layers, nseq, GN, GROUP_WIDTH), F32),
        ],
        scratch_shapes=[pltpu.VMEM((tq, LANES), F32)],
        input_output_aliases=aliases,
        compiler_params=_params(1),
        name="ssd_chunk",
    )(*args)


def _sample_ssd_intra_kernel(xs_ref, b_ref, c_ref, dt_ref, alog_ref, expand_ref, gexp_ref, dskip_ref,
                             ydiag_ref, xw_ref, ea_ref, cd_ref, *, t, nb):
    a_row = -jnp.exp(alog_ref[...])
    expand = expand_ref[...]
    gexp = gexp_ref[...]
    blk = lambda l: slice(l * nb, (l + 1) * nb)
    dts = [dt_ref[blk(l), :] for l in range(t)]
    acum = []
    run = None
    for l in range(t):
        step = dts[l] * a_row
        run = step if run is None else run + step
        acum.append(run)
    last = acum[t - 1]
    cd_ref[...] = _dot_split(jnp.exp(last), expand)
    for l in range(t):
        ea_ref[blk(l), :] = _dot_split(jnp.exp(acum[l]), expand)
        xs_l = _slab_cat(xs_ref, blk(l))
        xw_ref[blk(l), :] = xs_l * _dot_split(jnp.exp(last - acum[l]) * dts[l], expand)
        yd = dskip_ref[...] * xs_l
        c_l = _slab_cat(c_ref, blk(l))
        for s in range(l + 1):
            cbx = _dot((c_l * _slab_cat(b_ref, blk(s))).astype(BF16), gexp)
            w = _dot_split(jnp.exp(acum[l] - acum[s]) * dts[s], expand)
            yd = yd + cbx * w * _slab_cat(xs_ref, blk(s))
        ydiag_ref[blk(l), :] = yd


def _sample_ssd_intra_call(xs, bm, cm, dt, alog, expand, gexp, dskip, *, t, nb):
    rows = dt.shape[0]
    kern = functools.partial(_sample_ssd_intra_kernel, t=t, nb=nb)
    return pl.pallas_call(
        kern,
        in_specs=[_resident()] * 8,
        out_specs=[_resident()] * 4,
        out_shape=[
            jax.ShapeDtypeStruct((rows, D_INNER), F32),
            jax.ShapeDtypeStruct((rows, D_INNER), F32),
            jax.ShapeDtypeStruct((rows, D_INNER), F32),
            jax.ShapeDtypeStruct((nb, D_INNER), F32),
        ],
        compiler_params=pltpu.CompilerParams(vmem_limit_bytes=V7X_VMEM_LIMIT_BYTES),
        name="sample_ssd_intra",
    )(xs, bm, cm, dt, alog, expand, gexp, dskip)


def _sample_ssd_state_kernel(*refs, has_prev):
    h0_ref, c_ref, b_ref, xw_ref, ydiag_ref, ea_ref, cd_ref = refs[:7]
    y_ref, hnew_ref = refs[7 + has_prev], refs[8 + has_prev]
    for j in range(SAMPLE_SEQ_BLOCK):
        ht = h0_ref[j].T
        xw = xw_ref[:, j, :].astype(BF16)
        y_parts, new_parts = [], []
        for g in range(SSM_GROUPS):
            ws_ = slice(g * GROUP_WIDTH, (g + 1) * GROUP_WIDTH)
            y_parts.append(_dot(c_ref[g, :, j, :].astype(BF16), ht[:, ws_].astype(BF16)))
            new_parts.append(_dot_tn(b_ref[g, :, j, :].astype(BF16), xw[:, ws_]))
        y_ref[:, j, :] = ydiag_ref[:, j, :] + ea_ref[:, j, :] * jnp.concatenate(y_parts, axis=1)
        hnew = cd_ref[j:j + 1, :] * ht + jnp.concatenate(new_parts, axis=1)
        hnew_ref[j] = hnew.T


def _sample_ssd_state_call(h0_all, cm, bm, xw, ydiag, ea, cd, prev_state, *, layer):
    n_layers, nb = h0_all.shape[:2]
    t = xw.shape[0]
    bb = SAMPLE_SEQ_BLOCK
    has_prev = prev_state is not None
    kern = functools.partial(_sample_ssd_state_kernel, has_prev=int(has_prev))
    tspec = lambda w: pl.BlockSpec((t, bb, w), lambda i: (0, i, 0))
    gspec = pl.BlockSpec((G_SLABS, t, bb, LANES), lambda i: (0, 0, i, 0))
    state_spec = pl.BlockSpec((None, bb, D_INNER, SSM_STATE), lambda i: (layer, i, 0, 0))
    in_specs = [state_spec, gspec, gspec, tspec(D_INNER), tspec(D_INNER), tspec(D_INNER),
                pl.BlockSpec((bb, D_INNER), lambda i: (i, 0))]
    args = [h0_all, cm, bm, xw, ydiag, ea, cd]
    aliases = {}
    if has_prev:
        in_specs.append(pl.BlockSpec(memory_space=pl.ANY))
        args.append(prev_state)
        aliases = {len(args) - 1: 1}
    return pl.pallas_call(
        kern,
        grid=(nb // bb,),
        in_specs=in_specs,
        out_specs=[tspec(D_INNER), state_spec],
        out_shape=[
            jax.ShapeDtypeStruct((t, nb, D_INNER), F32),
            jax.ShapeDtypeStruct((n_layers, nb, D_INNER, SSM_STATE), F32),
        ],
        input_output_aliases=aliases,
        compiler_params=_params(1),
        name="sample_ssd_state",
    )(*args)


def _odd_out_kernel(y_ref, z_ref, x_ref, gn_ref, wout_ref, out_ref, *, y_slabs):
    y = _slab_cat(y_ref) if y_slabs else y_ref[...]
    v = y * _silu(z_ref[...])
    parts = []
    for g in range(SSM_GROUPS):
        vg = v[:, g * GROUP_WIDTH:(g + 1) * GROUP_WIDTH]
        ms = jnp.mean(vg * vg, axis=-1, keepdims=True)
        parts.append(vg * lax.rsqrt(ms + EPS))
    vn = (jnp.concatenate(parts, axis=1) * gn_ref[...]).astype(BF16)
    out_ref[...] = x_ref[...] + _dot(vn, _w(wout_ref))


def _odd_out_call(y, z, x, gn, wout, *, layer):
    rows = x.shape[0]
    tm = ROW_TILE
    y_slabs = y.ndim == 3
    row_spec = lambda w: pl.BlockSpec((tm, w), lambda i: (i, 0))
    slab_spec = pl.BlockSpec((X_SLABS, tm, LANES), lambda i: (0, i, 0))
    return pl.pallas_call(
        functools.partial(_odd_out_kernel, y_slabs=y_slabs),
        grid=(rows // tm,),
        in_specs=[slab_spec if y_slabs else row_spec(D_INNER), row_spec(D_INNER), row_spec(D_MODEL),
                  _resident(), _layer_weight(wout, layer)],
        out_specs=row_spec(D_MODEL),
        out_shape=jax.ShapeDtypeStruct((rows, D_MODEL), F32),
        compiler_params=_params(1),
        name="odd_out",
    )(y, z, x, gn, wout)


def _rope_tables(pos):
    half = ROPE_DIM // 2
    inv_freq = ROPE_THETA ** (-jnp.arange(0, ROPE_DIM, 2, dtype=F32) / ROPE_DIM)
    ang = pos.astype(F32)[:, None] * inv_freq[None, :]
    cos, sin = jnp.cos(ang), jnp.sin(ang)
    n = pos.shape[0]
    rest = HEAD_DIM - ROPE_DIM
    ra = jnp.concatenate([cos, cos, jnp.ones((n, rest), F32)], axis=1)
    rb = jnp.concatenate([jnp.zeros((n, half), F32), sin, jnp.zeros((n, rest), F32)], axis=1)
    rc = jnp.concatenate([-sin, jnp.zeros((n, half + rest), F32)], axis=1)
    rep = LANES // HEAD_DIM
    return tuple(jnp.tile(m, (1, rep)) for m in (ra, rb, rc))


def _head_mean_matrix():
    idx = jnp.arange(LANES) // HEAD_DIM
    return ((idx[:, None] == idx[None, :]).astype(F32) / HEAD_DIM).astype(BF16)


def _head_expand_matrix():
    idx = jnp.arange(D_INNER) // SSM_HEAD_DIM
    return (jnp.arange(SSM_HEADS)[:, None] == idx[None, :]).astype(BF16)


def _group_expand_matrix():
    src = jnp.arange(GN) // SSM_STATE
    dst = jnp.arange(D_INNER) // GROUP_WIDTH
    return (src[:, None] == dst[None, :]).astype(BF16)


def _make_tables(tp, bp, ts, nb):
    tril = jnp.tril(jnp.ones((SSD_CHUNK, SSD_CHUNK), BF16))
    return {
        "rope_p": _rope_tables(jnp.repeat(jnp.arange(tp, dtype=jnp.int32), bp)),
        "rope_s": _rope_tables(PAST_LEN + jnp.repeat(jnp.arange(ts, dtype=jnp.int32), nb)),
        "pmean": _stack2(_head_mean_matrix()),
        "expand": _stack2(_head_expand_matrix()),
        "gexp": _group_expand_matrix(),
        "tril": tril,
        "triu": tril.T,
    }


def _row(v):
    return v.reshape(1, -1)


def _time_major_hist(state):
    return jnp.swapaxes(state, 0, 1).reshape(-1, state.shape[-1])


def _batch_major_hist(rows, nseq):
    return jnp.swapaxes(rows.reshape(-1, nseq, rows.shape[-1]), 0, 1)


def _even_layer(xp, xs, conv_hist_s, k_hist_all, v_hist_all, layer, norm_g, w_in, conv_w, conv_b, ln_g, ln_b,
                qn_g, kn_g, sinks, w_out, tables, *, bp, tp, nb, ts):
    shared = (_row(norm_g), w_in, conv_w, _row(conv_b), _row(ln_g), _row(ln_b),
              _row(jnp.tile(qn_g, LANES // HEAD_DIM)), _row(jnp.tile(kn_g, LANES // HEAD_DIM)), tables["pmean"])
    keep = CONV_A_WIDTH - 1
    wout = w_out

    hist_p = jnp.zeros((keep * bp, CONV_A_DIM), F32)
    if xp.ndim == 3:
        c, q, k, v, hout, xp = _even_pre_call(xp, hist_p, *tables["rope_p"], *shared, stride=bp, layer=layer,
                                              in_seqs=bp)
    else:
        c, q, k, v, hout = _even_pre_call(xp, hist_p, *tables["rope_p"], *shared, stride=bp, layer=layer)
    xp_new = _attn_out_call(sinks, q, k, v, c, xp, wout, nseq=bp, layer=layer)
    conv_p = _batch_major_hist(hout, bp)
    last = lambda a: _batch_major_hist(a[-WINDOW * bp:], bp).reshape(bp, WINDOW, N_KV_HEADS, HEAD_DIM)
    wk_p, wv_p = last(k), last(v)

    c, q, k, v, hout = _even_pre_call(xs, _time_major_hist(conv_hist_s), *tables["rope_s"], *shared, stride=nb,
                                      layer=layer)
    conv_s = _batch_major_hist(hout, nb)
    k_new = _batch_major_hist(k, nb)
    v_new = _batch_major_hist(v, nb)
    per_slab = LANES // HEAD_DIM
    q_b = q.reshape(N_KV_HEADS, Q_SLABS // N_KV_HEADS, ts, nb, per_slab, HEAD_DIM).transpose(3, 0, 1, 4, 2, 5)
    q_b = q_b.reshape(nb, N_KV_HEADS, Q_PER_KV * ts, HEAD_DIM)
    kh_all = k_hist_all.reshape(-1, nb, WINDOW, KD)
    vh_all = v_hist_all.reshape(-1, nb, WINDOW, KD)
    o = _sample_attn_call(sinks, q_b, k_new, v_new, kh_all, vh_all, layer)
    o = o.reshape(nb, N_KV_HEADS, Q_PER_KV, ts, HEAD_DIM).transpose(3, 0, 1, 2, 4).reshape(ts * nb, QD)
    xs_new = _even_out_call(c, o, xs, wout, layer=layer)
    win_s = lambda hist, new: jnp.concatenate([hist[layer], new], axis=1)[:, -WINDOW:].reshape(
        nb, WINDOW, N_KV_HEADS, HEAD_DIM)
    return xp_new, xs_new, conv_p, conv_s, wk_p, win_s(kh_all, k_new), wv_p, win_s(vh_all, v_new)


def _odd_layer(xp, xs, conv_hist_s, ssm_all, layer, state_p, state_s, norm_g, w_in, conv_w, conv_b, dt_bias, a_log,
               d_skip, gn_g, w_out, tables, *, bp, tp, nb, ts):
    dskip = _row(jnp.repeat(d_skip, SSM_HEAD_DIM))
    gn = _row(gn_g)
    alog = _row(a_log)
    pre_shared = (_row(norm_g), w_in, conv_w, _row(conv_b), _row(dt_bias))
    keep = SSM_CONV_WIDTH - 1
    n_layers = ssm_all.shape[0]

    z, xc, bm, cm, dt, hout = _odd_pre_call(xp, jnp.zeros((keep * bp, SSM_CONV_DIM), F32), *pre_shared, stride=bp,
                                            layer=layer)
    y, state_p = _ssd_call(xc, bm, cm, dt, alog, a_log.reshape(-1, 1), tables["tril"], tables["triu"],
                           tables["expand"], dskip, state_p, nseq=bp, layer=layer, n_layers=n_layers)
    xp_new = _odd_out_call(y, z, xp, gn, w_out, layer=layer)
    conv_p = _batch_major_hist(hout, bp)

    z, xc, bm, cm, dt, hout = _odd_pre_call(xs, _time_major_hist(conv_hist_s), *pre_shared, stride=nb, layer=layer)
    conv_s = _batch_major_hist(hout, nb)
    ydiag, xw, ea, cd = _sample_ssd_intra_call(xc, bm, cm, dt, alog, tables["expand"], tables["gexp"], dskip,
                                               t=ts, nb=nb)
    tm3 = lambda a: a.reshape(ts, nb, a.shape[-1])
    slab4 = lambda a: a.reshape(G_SLABS, ts, nb, LANES)
    y, state_s = _sample_ssd_state_call(ssm_all, slab4(cm), slab4(bm), tm3(xw), tm3(ydiag), tm3(ea), cd, state_s,
                                        layer=layer)
    xs_new = _odd_out_call(y.reshape(ts * nb, D_INNER), z, xs, gn, w_out, layer=layer)
    return xp_new, xs_new, conv_p, conv_s, state_p, state_s


def _ffn_layer(xp, xs, hist_s, layer, norm_g, w_gate, w_up, conv_w, conv_b, w_down, *, bp, nb, last=False):
    keep = FFN_CONV_WIDTH - 1
    shared = (_row(norm_g), w_gate, w_up, conv_w, _row(conv_b), w_down)
    xp_new, hout = _ffn_call(xp, jnp.zeros((keep * bp, D_FF), F32), *shared, stride=bp, layer=layer,
                             out_seqs=bp if last else None)
    ffn_p = _batch_major_hist(hout, bp)
    xs_new, hout = _ffn_call(xs, _time_major_hist(hist_s), *shared, stride=nb, layer=layer)
    ffn_s = _batch_major_hist(hout, nb)
    return xp_new, xs_new, ffn_p, ffn_s


def kernel(x_prompt, x_sample, state_conv_a, cache_win_k, cache_win_v, state_conv_c, state_ssm, state_ffn_conv,
           norm_mix_e, w_in_e, conv_a_w, conv_a_b, ln_a_g, ln_a_b, q_norm_g, k_norm_g, sinks, w_out_e,
           norm_mix_o, w_in_o, conv_c_w, conv_c_b, dt_bias, a_log, d_skip, gnorm_c, w_out_o,
           norm_ffn, w_gate, w_up, ffn_conv_w, ffn_conv_b, w_down):
    bp, tp, _ = x_prompt.shape
    nb, ts, _ = x_sample.shape
    depth = norm_ffn.shape[0]
    n_odd = state_ssm.shape[0]
    tables = _make_tables(tp, bp, ts, nb)
    dims = dict(bp=bp, tp=tp, nb=nb, ts=ts)

    w_gate, w_up, w_down = _cast_bf16_call(w_gate), _cast_bf16_call(w_up), _cast_bf16_call(w_down)
    w_in_e, w_out_e = _cast_bf16_call(w_in_e), _cast_bf16_call(w_out_e)
    w_in_o, w_out_o = _cast_bf16_call(w_in_o), _cast_bf16_call(w_out_o)
    xp = x_prompt
    xs = jnp.swapaxes(x_sample, 0, 1).reshape(ts * nb, D_MODEL)
    ssm_all = state_ssm.reshape(n_odd, nb, D_INNER, SSM_STATE)
    state_p = state_s = None
    ca_p, ca_s, wk_p, wk_s, wv_p, wv_s = [], [], [], [], [], []
    cc_p, cc_s, ff_p, ff_s = [], [], [], []
    for layer in range(depth):
        i = layer // 2
        if layer % 2 == 0:
            xp, xs, c1, c2, k1, k2, v1, v2 = _even_layer(
                xp, xs, state_conv_a[i], cache_win_k, cache_win_v, i, norm_mix_e[i], w_in_e, conv_a_w[i],
                conv_a_b[i], ln_a_g[i], ln_a_b[i], q_norm_g[i], k_norm_g[i], sinks[i], w_out_e, tables, **dims)
            ca_p.append(c1); ca_s.append(c2); wk_p.append(k1); wk_s.append(k2); wv_p.append(v1); wv_s.append(v2)
        else:
            xp, xs, c1, c2, state_p, state_s = _odd_layer(
                xp, xs, state_conv_c[i], ssm_all, i, state_p, state_s, norm_mix_o[i], w_in_o, conv_c_w[i],
                conv_c_b[i], dt_bias[i], a_log[i], d_skip[i], gnorm_c[i], w_out_o, tables, **dims)
            cc_p.append(c1); cc_s.append(c2)
        xp, xs, f1, f2 = _ffn_layer(xp, xs, state_ffn_conv[layer], layer, norm_ffn[layer], w_gate, w_up,
                                    ffn_conv_w[layer], ffn_conv_b[layer], w_down, bp=bp, nb=nb,
                                    last=layer == depth - 1)
        ff_p.append(f1); ff_s.append(f2)

    y_prompt = xp
    y_sample = jnp.swapaxes(xs.reshape(ts, nb, D_MODEL), 0, 1)
    ssm_p = state_p.reshape(n_odd, bp, SSM_GROUPS, SSM_STATE, HEADS_PER_GROUP, SSM_HEAD_DIM)
    ssm_p = ssm_p.transpose(0, 1, 2, 4, 5, 3).reshape(n_odd, bp, SSM_HEADS, SSM_HEAD_DIM, SSM_STATE)
    ssm_s = state_s.reshape(n_odd, nb, SSM_HEADS, SSM_HEAD_DIM, SSM_STATE)
    return (y_prompt, y_sample,
            jnp.stack(ca_p), jnp.stack(ca_s),
            jnp.stack(wk_p), jnp.stack(wk_s),
            jnp.stack(wv_p), jnp.stack(wv_s),
            jnp.stack(cc_p), jnp.stack(cc_s),
            ssm_p, ssm_s,
            jnp.stack(ff_p), jnp.stack(ff_s))
```

```python
import functools

import jax
import jax.numpy as jnp
from jax import lax
from jax.experimental import pallas as pl
from jax.experimental.pallas import tpu as pltpu

F32 = jnp.float32
BF16 = jnp.bfloat16

D_MODEL = 1024
PAST_LEN = 8192
CONV_A_DIM = 512
CONV_A_WIDTH = 31
HEAD_DIM = 64
N_Q_HEADS = 8
N_KV_HEADS = 2
Q_PER_KV = N_Q_HEADS // N_KV_HEADS
QD = N_Q_HEADS * HEAD_DIM
KD = N_KV_HEADS * HEAD_DIM
WINDOW = 128
ROPE_DIM = 16
ROPE_THETA = 500000.0
D_INNER = 2048
SSM_HEAD_DIM = 64
SSM_HEADS = 32
SSM_STATE = 128
SSM_GROUPS = 4
HEADS_PER_GROUP = SSM_HEADS // SSM_GROUPS
GROUP_WIDTH = D_INNER // SSM_GROUPS
GN = SSM_GROUPS * SSM_STATE
SSM_CONV_WIDTH = 4
SSM_CONV_DIM = D_INNER + 2 * GN
SSD_CHUNK = 128
D_FF = 2816
FFN_CONV_WIDTH = 3
EPS = 1e-6

V7X_VMEM_LIMIT_BYTES = 58 * 1024 * 1024
SUBLANES = 8
LANES = 128
Q_SLABS = QD // LANES
X_SLABS = D_INNER // LANES
G_SLABS = GN // LANES
SLABS_PER_GROUP = GROUP_WIDTH // LANES

ROW_TILE = 512
SEQ_UNROLL = 2
SAMPLE_SEQ_BLOCK = 8


def _params(n_grid_dims):
    return pltpu.CompilerParams(
        dimension_semantics=("arbitrary",) * n_grid_dims,
        vmem_limit_bytes=V7X_VMEM_LIMIT_BYTES,
    )


def _resident():
    return pl.BlockSpec(memory_space=pltpu.VMEM)


def _layer_weight(stacked, layer):
    _, k, n = stacked.shape
    return pl.BlockSpec((None, k, n), lambda i: (layer, 0, 0), pipeline_mode=pl.Buffered(1))


def _w(ref, rows=slice(None), cols=slice(None)):
    return ref[rows, cols].astype(BF16)


def _row_tile(stride):
    return ROW_TILE if stride <= SUBLANES else ROW_TILE // 2


def _hist_spec(hb, width):
    return pl.BlockSpec((hb, width), lambda i: (0, 0), pipeline_mode=pl.Buffered(1))


def _rms(x, g_row):
    ms = jnp.mean(x * x, axis=-1, keepdims=True)
    return (x * lax.rsqrt(ms + EPS)) * g_row


def _sigmoid(x):
    return 1.0 / (1.0 + jnp.exp(-x))


def _silu(x):
    return x * _sigmoid(x)


def _softplus(x):
    return jnp.maximum(x, 0.0) + jnp.log(1.0 + jnp.exp(-jnp.abs(x)))


def _dot(a, b):
    return jnp.dot(a, b, preferred_element_type=F32)


def _dot_nt(a, b):
    return lax.dot_general(a, b, (((1,), (1,)), ((), ())), preferred_element_type=F32)


def _dot_tn(a, b):
    return lax.dot_general(a, b, (((0,), (0,)), ((), ())), preferred_element_type=F32)


def _bf16_terms(a, n):
    terms, rest = [], a
    for _ in range(n):
        t = rest.astype(BF16)
        terms.append(t)
        rest = rest - t.astype(F32)
    return terms


def _dot_split(a, m2_bf16):
    return _dot(jnp.concatenate(_bf16_terms(a, 2), axis=1), m2_bf16)


def _stack2(m):
    return jnp.concatenate([m, m], axis=0)


def _cumsum_rows(tril_bf16, x):
    n = x.shape[1]
    y = _dot(tril_bf16, jnp.concatenate(_bf16_terms(x, 3), axis=1))
    return y[:, 0:n] + y[:, n:2 * n] + y[:, 2 * n:3 * n]


def _cumsum_cols(x, triu_bf16):
    n = x.shape[0]
    y = _dot(jnp.concatenate(_bf16_terms(x, 3), axis=0), triu_bf16)
    return y[0:n] + y[n:2 * n] + y[2 * n:3 * n]


def _rope(x, ra, rb, rc):
    return x * ra + pltpu.roll(x, 8, 1) * rb + pltpu.roll(x, LANES - 8, 1) * rc


def _softmax_with_sink(s, sink):
    m = jnp.maximum(jnp.max(s, axis=-1, keepdims=True), sink)
    p = jnp.exp(s - m)
    den = jnp.sum(p, axis=-1, keepdims=True) + jnp.exp(sink - m)
    return p / den


def _slab_cat(ref, rows=slice(None)):
    return jnp.concatenate([ref[j, rows, :] for j in range(ref.shape[0])], axis=1)


def _cast_kernel(src_ref, dst_ref):
    dst_ref[...] = src_ref[...].astype(dst_ref.dtype)


def _cast_bf16_call(stacked):
    n_layers, k, n = stacked.shape
    kb = k // 4
    spec = pl.BlockSpec((None, kb, n), lambda l, i: (l, i, 0))
    return pl.pallas_call(
        _cast_kernel,
        grid=(n_layers, k // kb),
        in_specs=[spec],
        out_specs=spec,
        out_shape=jax.ShapeDtypeStruct(stacked.shape, BF16),
        compiler_params=_params(2),
        name="cast_bf16",
    )(stacked)


def _ffn_kernel(x_ref, hist_ref, g_ref, wg_ref, wu_ref, cw_ref, cb_ref, wd_ref,
                out_ref, hout_ref, gbuf, act, *scratch, tm, hb, stride, cw, out_seqs):
    @pl.when(pl.program_id(0) == 0)
    def _():
        gbuf[0:hb, :] = hist_ref[...]

    x = x_ref[...]
    h = _rms(x, g_ref[...]).astype(BF16)
    for j in range(D_FF // cw):
        cs = slice(j * cw, (j + 1) * cw)
        gate = _dot(h, _w(wg_ref, cols=cs))
        up = _dot(h, _w(wu_ref, cols=cs))
        gbuf[hb:hb + tm, cs] = gate
        conv = (cw_ref[0:1, cs] * gbuf[hb - 2 * stride:hb - 2 * stride + tm, cs]
                + cw_ref[1:2, cs] * gbuf[hb - stride:hb - stride + tm, cs]
                + cw_ref[2:3, cs] * gate + cb_ref[:, cs])
        act[:, cs] = (_silu(conv) * up).astype(BF16)
    y = x + _dot(act[...], _w(wd_ref))
    if out_seqs is None:
        out_ref[...] = y
    else:
        (yslab,) = scratch
        for j in range(D_MODEL // LANES):
            yslab[j] = y[:, j * LANES:(j + 1) * LANES]
        for b in range(out_seqs):
            out_ref[b] = _slab_cat(yslab, pl.ds(b, tm // out_seqs, stride=out_seqs))
    tail = gbuf[tm:tm + hb, :]
    hout_ref[...] = tail
    gbuf[0:hb, :] = tail


def _ffn_call(x, hist, g, wg, wu, cwt, cbias, wd, *, stride, layer, out_seqs=None):
    rows = x.shape[0]
    tm = _row_tile(stride)
    hb = (FFN_CONV_WIDTH - 1) * stride
    kern = functools.partial(_ffn_kernel, tm=tm, hb=hb, stride=stride, cw=256, out_seqs=out_seqs)
    scratch = [pltpu.VMEM((hb + tm, D_FF), F32), pltpu.VMEM((tm, D_FF), BF16)]
    if out_seqs is None:
        out_spec = pl.BlockSpec((tm, D_MODEL), lambda i: (i, 0))
        out_shape = jax.ShapeDtypeStruct((rows, D_MODEL), F32)
    else:
        steps = tm // out_seqs
        out_spec = pl.BlockSpec((out_seqs, steps, D_MODEL), lambda i: (0, i, 0))
        out_shape = jax.ShapeDtypeStruct((out_seqs, rows // out_seqs, D_MODEL), F32)
        scratch.append(pltpu.VMEM((D_MODEL // LANES, tm, LANES), F32))
    return pl.pallas_call(
        kern,
        grid=(rows // tm,),
        in_specs=[
            pl.BlockSpec((tm, D_MODEL), lambda i: (i, 0)),
            _hist_spec(hb, D_FF),
            _resident(), _layer_weight(wg, layer), _layer_weight(wu, layer), _resident(), _resident(),
            _layer_weight(wd, layer),
        ],
        out_specs=[out_spec, _hist_spec(hb, D_FF)],
        out_shape=[out_shape, jax.ShapeDtypeStruct((hb, D_FF), F32)],
        scratch_shapes=scratch,
        compiler_params=_params(1),
        name="conv_ffn",
    )(x, hist, g, wg, wu, cwt, cbias, wd)


CONV_TILES = 4


def _even_pre_kernel(x_ref, hist_ref, ra_ref, rb_ref, rc_ref, g_ref, win_ref, cw_ref, cb_ref,
                     lng_ref, lnb_ref, qg_ref, kg_ref, pmean_ref,
                     c_ref, q_ref, k_ref, v_ref, hout_ref, *rest, tm, hb, stride, in_seqs):
    if in_seqs is None:
        ubuf, cbuf = rest
        x = x_ref[...]
    else:
        xtm_ref, ubuf, cbuf, xslab = rest
        for b in range(in_seqs):
            for j in range(D_MODEL // LANES):
                xslab[j, pl.ds(b, tm // in_seqs, stride=in_seqs), :] = x_ref[b, :, j * LANES:(j + 1) * LANES]
        x = _slab_cat(xslab)
        xtm_ref[...] = x

    @pl.when(pl.program_id(0) == 0)
    def _():
        ubuf[0:hb, :] = hist_ref[...]

    h = _rms(x, g_ref[...]).astype(BF16)
    proj = _dot(h, _w(win_ref))
    u = proj[:, 0:CONV_A_DIM] * _sigmoid(proj[:, CONV_A_DIM:2 * CONV_A_DIM])
    ubuf[hb:hb + tm, :] = u

    tiles_per_step = stride // SUBLANES
    for c in range(CONV_A_DIM // LANES):
        cl = slice(c * LANES, (c + 1) * LANES)
        taps = [jnp.broadcast_to(cw_ref[j:j + 1, cl], (SUBLANES, LANES)) for j in range(CONV_A_WIDTH)]
        bias = jnp.broadcast_to(cb_ref[:, cl], (SUBLANES, LANES))
        for i0 in range(0, tm // SUBLANES, CONV_TILES):
            loaded = {}

            def u_tile(i):
                if i not in loaded:
                    loaded[i] = ubuf[i * SUBLANES:(i + 1) * SUBLANES, cl]
                return loaded[i]

            acc = [jnp.zeros((SUBLANES, LANES), F32)] * CONV_TILES
            for j in range(CONV_A_WIDTH):
                for a in range(CONV_TILES):
                    acc[a] = acc[a] + taps[j] * u_tile(i0 + a + j * tiles_per_step)
            for a in range(CONV_TILES):
                cbuf[(i0 + a) * SUBLANES:(i0 + a + 1) * SUBLANES, cl] = acc[a] + bias

    conv = cbuf[...]
    mu = jnp.mean(conv, axis=-1, keepdims=True)
    cen = conv - mu
    var = jnp.mean(cen * cen, axis=-1, keepdims=True)
    ln = cen * lax.rsqrt(var + EPS) * lng_ref[...] + lnb_ref[...]
    c_ref[...] = _silu(ln).astype(BF16)

    tail = ubuf[tm:tm + hb, :]
    hout_ref[...] = tail
    ubuf[0:hb, :] = tail

    ra, rb, rc = ra_ref[...], rb_ref[...], rc_ref[...]
    pmean = pmean_ref[...]
    q0 = 2 * CONV_A_DIM
    for s in range(Q_SLABS):
        qs = proj[:, q0 + s * LANES:q0 + (s + 1) * LANES]
        ms = _dot_split(qs * qs, pmean)
        qn = qs * lax.rsqrt(ms + EPS) * qg_ref[...]
        q_ref[s] = _rope(qn, ra, rb, rc)
    ks = proj[:, q0 + QD:q0 + QD + KD]
    ms = _dot_split(ks * ks, pmean)
    kn = ks * lax.rsqrt(ms + EPS) * kg_ref[...]
    k_ref[...] = _rope(kn, ra, rb, rc)
    v_ref[...] = proj[:, q0 + QD + KD:q0 + QD + 2 * KD]


def _even_pre_call(x, hist, ra, rb, rc, g, win, cwt, cbias, lng, lnb, qg, kg, pmean, *, stride, layer,
                   in_seqs=None):
    rows = x.shape[0] if in_seqs is None else x.shape[0] * x.shape[1]
    tm = _row_tile(stride)
    hb = (CONV_A_WIDTH - 1) * stride
    kern = functools.partial(_even_pre_kernel, tm=tm, hb=hb, stride=stride, in_seqs=in_seqs)
    row_spec = lambda w: pl.BlockSpec((tm, w), lambda i: (i, 0))
    x_spec = row_spec(D_MODEL)
    extra_specs, extra_shapes = [], []
    scratch = [pltpu.VMEM((hb + tm, CONV_A_DIM), F32), pltpu.VMEM((tm, CONV_A_DIM), F32)]
    if in_seqs is not None:
        x_spec = pl.BlockSpec((in_seqs, tm // in_seqs, D_MODEL), lambda i: (0, i, 0))
        extra_specs, extra_shapes = [row_spec(D_MODEL)], [jax.ShapeDtypeStruct((rows, D_MODEL), F32)]
        scratch.append(pltpu.VMEM((D_MODEL // LANES, tm, LANES), F32))
    return pl.pallas_call(
        kern,
        grid=(rows // tm,),
        in_specs=[
            x_spec,
            _hist_spec(hb, CONV_A_DIM),
            row_spec(LANES), row_spec(LANES), row_spec(LANES),
            _resident(), _layer_weight(win, layer),
        ] + [_resident()] * 7,
        out_specs=[
            row_spec(CONV_A_DIM),
            pl.BlockSpec((Q_SLABS, tm, LANES), lambda i: (0, i, 0)),
            row_spec(KD), row_spec(KD),
            _hist_spec(hb, CONV_A_DIM),
        ] + extra_specs,
        out_shape=[
            jax.ShapeDtypeStruct((rows, CONV_A_DIM), BF16),
            jax.ShapeDtypeStruct((Q_SLABS, rows, LANES), F32),
            jax.ShapeDtypeStruct((rows, KD), F32),
            jax.ShapeDtypeStruct((rows, KD), F32),
            jax.ShapeDtypeStruct((hb, CONV_A_DIM), F32),
        ] + extra_shapes,
        scratch_shapes=scratch,
        compiler_params=_params(1),
        name="even_pre",
    )(x, hist, ra, rb, rc, g, win, cwt, cbias, lng, lnb, qg, kg, pmean)


def _attn_out_kernel(sink_ref, q_ref, k_ref, v_ref, c_ref, x_ref, wout_ref,
                     out_ref, kprev, vprev, obuf, *, nseq):
    step = pl.program_id(0)

    @pl.when(step == 0)
    def _():
        kprev[...] = jnp.zeros_like(kprev)
        vprev[...] = jnp.zeros_like(vprev)

    row = lax.broadcasted_iota(jnp.int32, (WINDOW, 2 * WINDOW), 0)
    col = lax.broadcasted_iota(jnp.int32, (WINDOW, 2 * WINDOW), 1)
    valid = (col - row >= 1) & (col - row <= WINDOW) & ((col >= WINDOW) | (step > 0))

    def per_seq(b, carry):
        rows = pl.ds(b, WINDOW, stride=nseq)
        kb = k_ref[rows, :].astype(BF16)
        vb = v_ref[rows, :].astype(BF16)
        kk = jnp.concatenate([kprev[b], kb], axis=0)
        vv = jnp.concatenate([vprev[b], vb], axis=0)
        for j in range(Q_SLABS):
            qj = (q_ref[j, rows, :] * (HEAD_DIM ** -0.5)).astype(BF16)
            outs = []
            for h2 in range(LANES // HEAD_DIM):
                hd = j * (LANES // HEAD_DIM) + h2
                ls = slice((hd // Q_PER_KV) * HEAD_DIM, (hd // Q_PER_KV + 1) * HEAD_DIM)
                s = _dot_nt(qj[:, h2 * HEAD_DIM:(h2 + 1) * HEAD_DIM], kk[:, ls])
                s = jnp.where(valid, s, -jnp.inf)
                p = _softmax_with_sink(s, sink_ref[hd]).astype(BF16)
                outs.append(_dot(p, vv[:, ls]))
            obuf[j, rows, :] = jnp.concatenate(outs, axis=1)
        kprev[b] = kb
        vprev[b] = vb
        return carry

    lax.fori_loop(0, nseq, per_seq, 0, unroll=SEQ_UNROLL)
    o = _slab_cat(obuf).astype(BF16)
    mixed = (_dot(c_ref[...], _w(wout_ref, rows=slice(0, CONV_A_DIM)))
             + _dot(o, _w(wout_ref, rows=slice(CONV_A_DIM, None))))
    out_ref[...] = x_ref[...] + mixed


def _attn_out_call(sinks, q, k, v, c, x, wout, *, nseq, layer):
    rows = x.shape[0]
    tq = WINDOW * nseq
    kern = functools.partial(_attn_out_kernel, nseq=nseq)
    row_spec = lambda w: pl.BlockSpec((tq, w), lambda i: (i, 0))
    return pl.pallas_call(
        kern,
        grid=(rows // tq,),
        in_specs=[
            pl.BlockSpec(memory_space=pltpu.SMEM),
            pl.BlockSpec((Q_SLABS, tq, LANES), lambda i: (0, i, 0)),
            row_spec(KD), row_spec(KD), row_spec(CONV_A_DIM), row_spec(D_MODEL), _layer_weight(wout, layer),
        ],
        out_specs=row_spec(D_MODEL),
        out_shape=jax.ShapeDtypeStruct((rows, D_MODEL), F32),
        scratch_shapes=[pltpu.VMEM((nseq, WINDOW, KD), BF16), pltpu.VMEM((nseq, WINDOW, KD), BF16),
                        pltpu.VMEM((Q_SLABS, tq, LANES), F32)],
        compiler_params=_params(1),
        name="attn_out",
    )(sinks, q, k, v, c, x, wout)


def _sample_attn_kernel(sink_ref, q_ref, kn_ref, vn_ref, kh_ref, vh_ref, o_ref, *, t):
    bb = SAMPLE_SEQ_BLOCK
    rows = Q_PER_KV * t
    pitch = 2 * WINDOW
    pad = jnp.zeros((pitch - WINDOW - t, KD), F32)
    kall = jnp.concatenate([p for b in range(bb) for p in (kh_ref[b], kn_ref[b], pad)], axis=0).astype(BF16)
    vall = jnp.concatenate([p for b in range(bb) for p in (vh_ref[b], vn_ref[b], pad)], axis=0).astype(BF16)
    r = lax.broadcasted_iota(jnp.int32, (bb * rows, pitch), 0)
    j = lax.broadcasted_iota(jnp.int32, (bb * rows, pitch), 1)
    tq = r % t
    valid = (j > tq) & (j <= tq + WINDOW)
    grp = (lax.broadcasted_iota(jnp.int32, (bb * rows, 1), 0) % rows) // t
    same_seq = (lax.broadcasted_iota(jnp.int32, (bb * rows, bb * pitch), 0) // rows
                == lax.broadcasted_iota(jnp.int32, (bb * rows, bb * pitch), 1) // pitch)
    for kv in range(N_KV_HEADS):
        sink = jnp.zeros((bb * rows, 1), F32)
        for g in range(Q_PER_KV):
            sink = jnp.where(grp == g, sink_ref[kv * Q_PER_KV + g], sink)
        ls = slice(kv * HEAD_DIM, (kv + 1) * HEAD_DIM)
        q = q_ref[:, kv].reshape(bb * rows, HEAD_DIM).astype(BF16)
        s_all = _dot_nt(q, kall[:, ls]) * (HEAD_DIM ** -0.5)
        s = jnp.concatenate([s_all[b * rows:(b + 1) * rows, b * pitch:(b + 1) * pitch] for b in range(bb)], axis=0)
        p = _softmax_with_sink(jnp.where(valid, s, -jnp.inf), sink)
        p_wide = jnp.where(same_seq, jnp.concatenate([p] * bb, axis=1), 0.0).astype(BF16)
        o_ref[:, kv] = _dot(p_wide, vall[:, ls]).reshape(bb, rows, HEAD_DIM)


def _sample_attn_call(sinks, q, kn, vn, kh, vh, layer):
    nb, _, rows, _ = q.shape
    t = kn.shape[1]
    bb = SAMPLE_SEQ_BLOCK
    kern = functools.partial(_sample_attn_kernel, t=t)
    hist_spec = pl.BlockSpec((None, bb, WINDOW, KD), lambda i: (layer, i, 0, 0))
    return pl.pallas_call(
        kern,
        grid=(nb // bb,),
        in_specs=[
            pl.BlockSpec(memory_space=pltpu.SMEM),
            pl.BlockSpec((bb, N_KV_HEADS, rows, HEAD_DIM), lambda i: (i, 0, 0, 0)),
            pl.BlockSpec((bb, t, KD), lambda i: (i, 0, 0)),
            pl.BlockSpec((bb, t, KD), lambda i: (i, 0, 0)),
            hist_spec, hist_spec,
        ],
        out_specs=pl.BlockSpec((bb, N_KV_HEADS, rows, HEAD_DIM), lambda i: (i, 0, 0, 0)),
        out_shape=jax.ShapeDtypeStruct((nb, N_KV_HEADS, rows, HEAD_DIM), F32),
        compiler_params=_params(1),
        name="sample_attn",
    )(sinks, q, kn, vn, kh, vh)


def _even_out_kernel(c_ref, o_ref, x_ref, wout_ref, out_ref):
    mixed = (_dot(c_ref[...], _w(wout_ref, rows=slice(0, CONV_A_DIM)))
             + _dot(o_ref[...].astype(BF16), _w(wout_ref, rows=slice(CONV_A_DIM, None))))
    out_ref[...] = x_ref[...] + mixed


def _even_out_call(c, o, x, wout, *, layer):
    rows = x.shape[0]
    tm = ROW_TILE
    row_spec = lambda w: pl.BlockSpec((tm, w), lambda i: (i, 0))
    return pl.pallas_call(
        _even_out_kernel,
        grid=(rows // tm,),
        in_specs=[row_spec(CONV_A_DIM), row_spec(QD), row_spec(D_MODEL), _layer_weight(wout, layer)],
        out_specs=row_spec(D_MODEL),
        out_shape=jax.ShapeDtypeStruct((rows, D_MODEL), F32),
        compiler_params=_params(1),
        name="even_out",
    )(c, o, x, wout)


def _odd_pre_kernel(x_ref, hist_ref, g_ref, win_ref, cw_ref, cb_ref, dtb_ref,
                    z_ref, xs_ref, b_ref, c_ref, dt_ref, hout_ref, xbuf, *, tm, hb, stride, cw):
    @pl.when(pl.program_id(0) == 0)
    def _():
        xbuf[0:hb, :] = hist_ref[...]

    h = _rms(x_ref[...], g_ref[...]).astype(BF16)
    for j in range(D_INNER // cw):
        cs = slice(j * cw, (j + 1) * cw)
        z_ref[:, cs] = _dot(h, _w(win_ref, cols=cs))
    spb = cw // LANES
    for j in range(SSM_CONV_DIM // cw):
        cs = slice(j * cw, (j + 1) * cw)
        pre = _dot(h, _w(win_ref, cols=slice(D_INNER + j * cw, D_INNER + (j + 1) * cw)))
        xbuf[hb:hb + tm, cs] = pre
        conv = cw_ref[SSM_CONV_WIDTH - 1:SSM_CONV_WIDTH, cs] * pre + cb_ref[:, cs]
        for tap in range(SSM_CONV_WIDTH - 1):
            off = hb - (SSM_CONV_WIDTH - 1 - tap) * stride
            conv = conv + cw_ref[tap:tap + 1, cs] * xbuf[off:off + tm, cs]
        act = _silu(conv)
        for q in range(spb):
            slab = j * spb + q
            piece = act[:, q * LANES:(q + 1) * LANES]
            if slab < X_SLABS:
                xs_ref[slab] = piece
            elif slab < X_SLABS + G_SLABS:
                b_ref[slab - X_SLABS] = piece
            else:
                c_ref[slab - X_SLABS - G_SLABS] = piece
    dt_cols = slice(D_INNER + SSM_CONV_DIM, D_INNER + SSM_CONV_DIM + SSM_HEADS)
    dt_ref[...] = _softplus(_dot(h, _w(win_ref, cols=dt_cols)) + dtb_ref[...])
    tail = xbuf[tm:tm + hb, :]
    hout_ref[...] = tail
    xbuf[0:hb, :] = tail


def _odd_pre_call(x, hist, g, win, cwt, cbias, dtb, *, stride, layer):
    rows = x.shape[0]
    tm = _row_tile(stride)
    hb = (SSM_CONV_WIDTH - 1) * stride
    kern = functools.partial(_odd_pre_kernel, tm=tm, hb=hb, stride=stride, cw=512)
    row_spec = lambda w: pl.BlockSpec((tm, w), lambda i: (i, 0))
    slab_spec = lambda n: pl.BlockSpec((n, tm, LANES), lambda i: (0, i, 0))
    return pl.pallas_call(
        kern,
        grid=(rows // tm,),
        in_specs=[
            row_spec(D_MODEL),
            _hist_spec(hb, SSM_CONV_DIM),
            _resident(), _layer_weight(win, layer), _resident(), _resident(), _resident(),
        ],
        out_specs=[
            row_spec(D_INNER), slab_spec(X_SLABS), slab_spec(G_SLABS), slab_spec(G_SLABS), row_spec(SSM_HEADS),
            _hist_spec(hb, SSM_CONV_DIM),
        ],
        out_shape=[
            jax.ShapeDtypeStruct((rows, D_INNER), F32),
            jax.ShapeDtypeStruct((X_SLABS, rows, LANES), F32),
            jax.ShapeDtypeStruct((G_SLABS, rows, LANES), F32),
            jax.ShapeDtypeStruct((G_SLABS, rows, LANES), F32),
            jax.ShapeDtypeStruct((rows, SSM_HEADS), F32),
            jax.ShapeDtypeStruct((hb, SSM_CONV_DIM), F32),
        ],
        scratch_shapes=[pltpu.VMEM((hb + tm, SSM_CONV_DIM), F32)],
        compiler_params=_params(1),
        name="odd_pre",
    )(x, hist, g, win, cwt, cbias, dtb)


def _ssd_kernel(*refs, nseq, has_prev):
    (xs_ref, b_ref, c_ref, dt_ref, alog_ref, alogt_ref, tril_ref, triu_ref, expand_ref, dskip_ref) = refs[:10]
    y_ref, ht = refs[10 + has_prev], refs[11 + has_prev]
    dtbuf = refs[12 + has_prev]
    L = SSD_CHUNK
    N = SSM_STATE
    step = pl.program_id(0)

    @pl.when(step == 0)
    def _():
        ht[...] = jnp.zeros_like(ht)
        dtbuf[...] = jnp.zeros_like(dtbuf)

    dtbuf[:, 0:SSM_HEADS] = dt_ref[...]
    a_row = -jnp.exp(alog_ref[...])
    a_col = -jnp.exp(alogt_ref[...])
    tril, triu, expand = tril_ref[...], triu_ref[...], expand_ref[...]
    row = lax.broadcasted_iota(jnp.int32, (L, L), 0)
    col = lax.broadcasted_iota(jnp.int32, (L, L), 1)
    causal = row >= col
    lane = lax.broadcasted_iota(jnp.int32, (L, LANES), 1)
    low_half = lane < SSM_HEAD_DIM

    def per_seq(b, carry):
        rows = pl.ds(b, L, stride=nseq)
        dt_wide = dtbuf[rows, :]
        dt = dt_wide[:, 0:SSM_HEADS]
        dtt = dt_wide.T[0:SSM_HEADS, :]
        acum = _cumsum_rows(tril, dt * a_row)
        acumt = _cumsum_cols(dtt * a_col, triu)
        last = acum[L - 1:L, :]
        ea = _dot_split(jnp.exp(acum), expand)
        de = _dot_split(jnp.exp(last - acum) * dt, expand)
        cd = _dot_split(jnp.broadcast_to(jnp.exp(last), (SUBLANES, SSM_HEADS)), expand)[0:1, :]

        cbs, y_off, b_t = [], [], []
        for g in range(SSM_GROUPS):
            cg = c_ref[g, rows, :].astype(BF16)
            bg = b_ref[g, rows, :]
            cbs.append(_dot_nt(cg, bg.astype(BF16)))
            y_off.append(_dot(cg, ht[b, g * N:(g + 1) * N, :].astype(BF16)))
            b_t.append(bg.T.astype(BF16))

        xw = []
        for slab in range(X_SLABS):
            g = slab // SLABS_PER_GROUP
            ws = []
            for hd in (2 * slab, 2 * slab + 1):
                diff = jnp.where(causal, acum[:, hd:hd + 1] - acumt[hd:hd + 1, :], -jnp.inf)
                ws.append(cbs[g] * jnp.exp(diff) * dtt[hd:hd + 1, :])
            wcat = jnp.concatenate(ws, axis=1).astype(BF16)
            xpair = xs_ref[slab, rows, :]
            xbd = jnp.concatenate([jnp.where(low_half, xpair, 0.0), jnp.where(low_half, 0.0, xpair)], axis=0)
            ls = slice(slab * LANES, (slab + 1) * LANES)
            gl = slice((slab % SLABS_PER_GROUP) * LANES, (slab % SLABS_PER_GROUP + 1) * LANES)
            y_ref[slab, rows, :] = (_dot(wcat, xbd.astype(BF16)) + ea[:, ls] * y_off[g][:, gl]
                                    + dskip_ref[:, ls] * xpair)
            xw.append((xpair * de[:, ls]).astype(BF16))

        for g in range(SSM_GROUPS):
            xw_g = jnp.concatenate(xw[g * SLABS_PER_GROUP:(g + 1) * SLABS_PER_GROUP], axis=1)
            gs = slice(g * N, (g + 1) * N)
            ht[b, gs, :] = cd[:, g * GROUP_WIDTH:(g + 1) * GROUP_WIDTH] * ht[b, gs, :] + _dot(b_t[g], xw_g)
        return carry

    lax.fori_loop(0, nseq, per_seq, 0)


def _ssd_call(xs, bm, cm, dt, alog, alogt, tril, triu, expand, dskip, prev_state, *, nseq, layer, n_layers):
    rows = dt.shape[0]
    tq = SSD_CHUNK * nseq
    has_prev = prev_state is not None
    kern = functools.partial(_ssd_kernel, nseq=nseq, has_prev=int(has_prev))
    slab_spec = lambda n: pl.BlockSpec((n, tq, LANES), lambda i: (0, i, 0))
    in_specs = [
        slab_spec(X_SLABS), slab_spec(G_SLABS), slab_spec(G_SLABS),
        pl.BlockSpec((tq, SSM_HEADS), lambda i: (i, 0)),
    ] + [_resident()] * 6
    args = [xs, bm, cm, dt, alog, alogt, tril, triu, expand, dskip]
    aliases = {}
    if has_prev:
        in_specs.append(pl.BlockSpec(memory_space=pl.ANY))
        args.append(prev_state)
        aliases = {len(args) - 1: 1}
    return pl.pallas_call(
        kern,
        grid=(rows // tq,),
        in_specs=in_specs,
        out_specs=[
            slab_spec(X_SLABS),
            pl.BlockSpec((None, nseq, GN, GROUP_WIDTH), lambda i: (layer, 0, 0, 0), pipeline_mode=pl.Buffered(1)),
        ],
        out_shape=[
            jax.ShapeDtypeStruct((X_SLABS, rows, LANES), F32),
            jax.ShapeDtypeStruct((n_layers, nseq, GN, GROUP_WIDTH), F32),
        ],
        scratch_shapes=[pltpu.VMEM((tq, LANES), F32)],
        input_output_aliases=aliases,
        compiler_params=_params(1),
        name="ssd_chunk",
    )(*args)


def _sample_ssd_intra_kernel(xs_ref, b_ref, c_ref, dt_ref, alog_ref, expand_ref, gexp_ref, dskip_ref,
                             ydiag_ref, xw_ref, ea_ref, cd_ref, *, t, nb):
    a_row = -jnp.exp(alog_ref[...])
    expand = expand_ref[...]
    gexp = gexp_ref[...]
    blk = lambda l: slice(l * nb, (l + 1) * nb)
    dts = [dt_ref[blk(l), :] for l in range(t)]
    acum = []
    run = None
    for l in range(t):
        step = dts[l] * a_row
        run = step if run is None else run + step
        acum.append(run)
    last = acum[t - 1]
    cd_ref[...] = jnp.exp(last)
    for l in range(t):
        ea_ref[blk(l), :] = _dot_split(jnp.exp(acum[l]), expand)
        xs_l = _slab_cat(xs_ref, blk(l))
        xw_ref[blk(l), :] = xs_l * _dot_split(jnp.exp(last - acum[l]) * dts[l], expand)
        yd = dskip_ref[...] * xs_l
        c_l = _slab_cat(c_ref, blk(l))
        for s in range(l + 1):
            cbx = _dot((c_l * _slab_cat(b_ref, blk(s))).astype(BF16), gexp)
            w = _dot_split(jnp.exp(acum[l] - acum[s]) * dts[s], expand)
            yd = yd + cbx * w * _slab_cat(xs_ref, blk(s))
        ydiag_ref[blk(l), :] = yd


def _sample_ssd_intra_call(xs, bm, cm, dt, alog, expand, gexp, dskip, *, t, nb):
    rows = dt.shape[0]
    kern = functools.partial(_sample_ssd_intra_kernel, t=t, nb=nb)
    return pl.pallas_call(
        kern,
        in_specs=[_resident()] * 8,
        out_specs=[_resident()] * 4,
        out_shape=[
            jax.ShapeDtypeStruct((rows, D_INNER), F32),
            jax.ShapeDtypeStruct((rows, D_INNER), F32),
            jax.ShapeDtypeStruct((rows, D_INNER), F32),
            jax.ShapeDtypeStruct((nb, SSM_HEADS), F32),
        ],
        compiler_params=pltpu.CompilerParams(vmem_limit_bytes=V7X_VMEM_LIMIT_BYTES),
        name="sample_ssd_intra",
    )(xs, bm, cm, dt, alog, expand, gexp, dskip)


def _sample_ssd_state_kernel(*refs, has_prev):
    cd_ref, h0_ref, c_ref, b_ref, xw_ref, ydiag_ref, ea_ref = refs[:7]
    y_ref, hnew_ref = refs[7 + has_prev], refs[8 + has_prev]
    first_seq = pl.program_id(0) * SAMPLE_SEQ_BLOCK
    for j in range(SAMPLE_SEQ_BLOCK):
        xw = xw_ref[:, j, :].astype(BF16)
        y_parts = []
        for g in range(SSM_GROUPS):
            ws_ = slice(g * GROUP_WIDTH, (g + 1) * GROUP_WIDTH)
            hg = h0_ref[j, ws_, :]
            y_parts.append(_dot_nt(c_ref[g, :, j, :].astype(BF16), hg.astype(BF16)))
            new = _dot_tn(xw[:, ws_], b_ref[g, :, j, :].astype(BF16))
            for r in range(HEADS_PER_GROUP):
                hd = g * HEADS_PER_GROUP + r
                hs = slice(r * SSM_HEAD_DIM, (r + 1) * SSM_HEAD_DIM)
                decay = cd_ref[(first_seq + j) * SSM_HEADS + hd]
                hnew_ref[j, hd * SSM_HEAD_DIM:(hd + 1) * SSM_HEAD_DIM, :] = decay * hg[hs, :] + new[hs, :]
        y_ref[:, j, :] = ydiag_ref[:, j, :] + ea_ref[:, j, :] * jnp.concatenate(y_parts, axis=1)


def _sample_ssd_state_call(h0_all, cm, bm, xw, ydiag, ea, cd, prev_state, *, layer):
    n_layers, nb = h0_all.shape[:2]
    t = xw.shape[0]
    bb = SAMPLE_SEQ_BLOCK
    has_prev = prev_state is not None
    kern = functools.partial(_sample_ssd_state_kernel, has_prev=int(has_prev))
    tspec = lambda w: pl.BlockSpec((t, bb, w), lambda i: (0, i, 0))
    gspec = pl.BlockSpec((G_SLABS, t, bb, LANES), lambda i: (0, 0, i, 0))
    state_spec = pl.BlockSpec((None, bb, D_INNER, SSM_STATE), lambda i: (layer, i, 0, 0))
    in_specs = [pl.BlockSpec(memory_space=pltpu.SMEM), state_spec, gspec, gspec,
                tspec(D_INNER), tspec(D_INNER), tspec(D_INNER)]
    args = [cd.reshape(-1), h0_all, cm, bm, xw, ydiag, ea]
    aliases = {}
    if has_prev:
        in_specs.append(pl.BlockSpec(memory_space=pl.ANY))
        args.append(prev_state)
        aliases = {len(args) - 1: 1}
    return pl.pallas_call(
        kern,
        grid=(nb // bb,),
        in_specs=in_specs,
        out_specs=[tspec(D_INNER), state_spec],
        out_shape=[
            jax.ShapeDtypeStruct((t, nb, D_INNER), F32),
            jax.ShapeDtypeStruct((n_layers, nb, D_INNER, SSM_STATE), F32),
        ],
        input_output_aliases=aliases,
        compiler_params=_params(1),
        name="sample_ssd_state",
    )(*args)


def _odd_out_kernel(y_ref, z_ref, x_ref, gn_ref, wout_ref, out_ref, *, y_slabs):
    y = _slab_cat(y_ref) if y_slabs else y_ref[...]
    v = y * _silu(z_ref[...])
    parts = []
    for g in range(SSM_GROUPS):
        vg = v[:, g * GROUP_WIDTH:(g + 1) * GROUP_WIDTH]
        ms = jnp.mean(vg * vg, axis=-1, keepdims=True)
        parts.append(vg * lax.rsqrt(ms + EPS))
    vn = (jnp.concatenate(parts, axis=1) * gn_ref[...]).astype(BF16)
    out_ref[...] = x_ref[...] + _dot(vn, _w(wout_ref))


def _odd_out_call(y, z, x, gn, wout, *, layer):
    rows = x.shape[0]
    tm = ROW_TILE
    y_slabs = y.ndim == 3
    row_spec = lambda w: pl.BlockSpec((tm, w), lambda i: (i, 0))
    slab_spec = pl.BlockSpec((X_SLABS, tm, LANES), lambda i: (0, i, 0))
    return pl.pallas_call(
        functools.partial(_odd_out_kernel, y_slabs=y_slabs),
        grid=(rows // tm,),
        in_specs=[slab_spec if y_slabs else row_spec(D_INNER), row_spec(D_INNER), row_spec(D_MODEL),
                  _resident(), _layer_weight(wout, layer)],
        out_specs=row_spec(D_MODEL),
        out_shape=jax.ShapeDtypeStruct((rows, D_MODEL), F32),
        compiler_params=_params(1),
        name="odd_out",
    )(y, z, x, gn, wout)


def _rope_tables(pos):
    half = ROPE_DIM // 2
    inv_freq = ROPE_THETA ** (-jnp.arange(0, ROPE_DIM, 2, dtype=F32) / ROPE_DIM)
    ang = pos.astype(F32)[:, None] * inv_freq[None, :]
    cos, sin = jnp.cos(ang), jnp.sin(ang)
    n = pos.shape[0]
    rest = HEAD_DIM - ROPE_DIM
    ra = jnp.concatenate([cos, cos, jnp.ones((n, rest), F32)], axis=1)
    rb = jnp.concatenate([jnp.zeros((n, half), F32), sin, jnp.zeros((n, rest), F32)], axis=1)
    rc = jnp.concatenate([-sin, jnp.zeros((n, half + rest), F32)], axis=1)
    rep = LANES // HEAD_DIM
    return tuple(jnp.tile(m, (1, rep)) for m in (ra, rb, rc))


def _head_mean_matrix():
    idx = jnp.arange(LANES) // HEAD_DIM
    return ((idx[:, None] == idx[None, :]).astype(F32) / HEAD_DIM).astype(BF16)


def _head_expand_matrix():
    idx = jnp.arange(D_INNER) // SSM_HEAD_DIM
    return (jnp.arange(SSM_HEADS)[:, None] == idx[None, :]).astype(BF16)


def _group_expand_matrix():
    src = jnp.arange(GN) // SSM_STATE
    dst = jnp.arange(D_INNER) // GROUP_WIDTH
    return (src[:, None] == dst[None, :]).astype(BF16)


def _make_tables(tp, bp, ts, nb):
    tril = jnp.tril(jnp.ones((SSD_CHUNK, SSD_CHUNK), BF16))
    return {
        "rope_p": _rope_tables(jnp.repeat(jnp.arange(tp, dtype=jnp.int32), bp)),
        "rope_s": _rope_tables(PAST_LEN + jnp.repeat(jnp.arange(ts, dtype=jnp.int32), nb)),
        "pmean": _stack2(_head_mean_matrix()),
        "expand": _stack2(_head_expand_matrix()),
        "gexp": _group_expand_matrix(),
        "tril": tril,
        "triu": tril.T,
    }


def _row(v):
    return v.reshape(1, -1)


def _time_major_hist(state):
    return jnp.swapaxes(state, 0, 1).reshape(-1, state.shape[-1])


def _batch_major_hist(rows, nseq):
    return jnp.swapaxes(rows.reshape(-1, nseq, rows.shape[-1]), 0, 1)


def _even_layer(xp, xs, conv_hist_s, k_hist_all, v_hist_all, layer, norm_g, w_in, conv_w, conv_b, ln_g, ln_b,
                qn_g, kn_g, sinks, w_out, tables, *, bp, tp, nb, ts):
    shared = (_row(norm_g), w_in, conv_w, _row(conv_b), _row(ln_g), _row(ln_b),
              _row(jnp.tile(qn_g, LANES // HEAD_DIM)), _row(jnp.tile(kn_g, LANES // HEAD_DIM)), tables["pmean"])
    keep = CONV_A_WIDTH - 1
    wout = w_out

    hist_p = jnp.zeros((keep * bp, CONV_A_DIM), F32)
    if xp.ndim == 3:
        c, q, k, v, hout, xp = _even_pre_call(xp, hist_p, *tables["rope_p"], *shared, stride=bp, layer=layer,
                                              in_seqs=bp)
    else:
        c, q, k, v, hout = _even_pre_call(xp, hist_p, *tables["rope_p"], *shared, stride=bp, layer=layer)
    xp_new = _attn_out_call(sinks, q, k, v, c, xp, wout, nseq=bp, layer=layer)
    conv_p = _batch_major_hist(hout, bp)
    last = lambda a: _batch_major_hist(a[-WINDOW * bp:], bp).reshape(bp, WINDOW, N_KV_HEADS, HEAD_DIM)
    wk_p, wv_p = last(k), last(v)

    c, q, k, v, hout = _even_pre_call(xs, _time_major_hist(conv_hist_s), *tables["rope_s"], *shared, stride=nb,
                                      layer=layer)
    conv_s = _batch_major_hist(hout, nb)
    k_new = _batch_major_hist(k, nb)
    v_new = _batch_major_hist(v, nb)
    per_slab = LANES // HEAD_DIM
    q_b = q.reshape(N_KV_HEADS, Q_SLABS // N_KV_HEADS, ts, nb, per_slab, HEAD_DIM).transpose(3, 0, 1, 4, 2, 5)
    q_b = q_b.reshape(nb, N_KV_HEADS, Q_PER_KV * ts, HEAD_DIM)
    kh_all = k_hist_all.reshape(-1, nb, WINDOW, KD)
    vh_all = v_hist_all.reshape(-1, nb, WINDOW, KD)
    o = _sample_attn_call(sinks, q_b, k_new, v_new, kh_all, vh_all, layer)
    o = o.reshape(nb, N_KV_HEADS, Q_PER_KV, ts, HEAD_DIM).transpose(3, 0, 1, 2, 4).reshape(ts * nb, QD)
    xs_new = _even_out_call(c, o, xs, wout, layer=layer)
    win_s = lambda hist, new: jnp.concatenate([hist[layer], new], axis=1)[:, -WINDOW:].reshape(
        nb, WINDOW, N_KV_HEADS, HEAD_DIM)
    return xp_new, xs_new, conv_p, conv_s, wk_p, win_s(kh_all, k_new), wv_p, win_s(vh_all, v_new)


def _odd_layer(xp, xs, conv_hist_s, ssm_all, layer, state_p, state_s, norm_g, w_in, conv_w, conv_b, dt_bias, a_log,
               d_skip, gn_g, w_out, tables, *, bp, tp, nb, ts):
    dskip = _row(jnp.repeat(d_skip, SSM_HEAD_DIM))
    gn = _row(gn_g)
    alog = _row(a_log)
    pre_shared = (_row(norm_g), w_in, conv_w, _row(conv_b), _row(dt_bias))
    keep = SSM_CONV_WIDTH - 1
    n_layers = ssm_all.shape[0]

    z, xc, bm, cm, dt, hout = _odd_pre_call(xp, jnp.zeros((keep * bp, SSM_CONV_DIM), F32), *pre_shared, stride=bp,
                                            layer=layer)
    y, state_p = _ssd_call(xc, bm, cm, dt, alog, a_log.reshape(-1, 1), tables["tril"], tables["triu"],
                           tables["expand"], dskip, state_p, nseq=bp, layer=layer, n_layers=n_layers)
    xp_new = _odd_out_call(y, z, xp, gn, w_out, layer=layer)
    conv_p = _batch_major_hist(hout, bp)

    z, xc, bm, cm, dt, hout = _odd_pre_call(xs, _time_major_hist(conv_hist_s), *pre_shared, stride=nb, layer=layer)
    conv_s = _batch_major_hist(hout, nb)
    ydiag, xw, ea, cd = _sample_ssd_intra_call(xc, bm, cm, dt, alog, tables["expand"], tables["gexp"], dskip,
                                               t=ts, nb=nb)
    tm3 = lambda a: a.reshape(ts, nb, a.shape[-1])
    slab4 = lambda a: a.reshape(G_SLABS, ts, nb, LANES)
    y, state_s = _sample_ssd_state_call(ssm_all, slab4(cm), slab4(bm), tm3(xw), tm3(ydiag), tm3(ea), cd, state_s,
                                        layer=layer)
    xs_new = _odd_out_call(y.reshape(ts * nb, D_INNER), z, xs, gn, w_out, layer=layer)
    return xp_new, xs_new, conv_p, conv_s, state_p, state_s


def _ffn_layer(xp, xs, hist_s, layer, norm_g, w_gate, w_up, conv_w, conv_b, w_down, *, bp, nb, last=False):
    keep = FFN_CONV_WIDTH - 1
    shared = (_row(norm_g), w_gate, w_up, conv_w, _row(conv_b), w_down)
    xp_new, hout = _ffn_call(xp, jnp.zeros((keep * bp, D_FF), F32), *shared, stride=bp, layer=layer,
                             out_seqs=bp if last else None)
    ffn_p = _batch_major_hist(hout, bp)
    xs_new, hout = _ffn_call(xs, _time_major_hist(hist_s), *shared, stride=nb, layer=layer)
    ffn_s = _batch_major_hist(hout, nb)
    return xp_new, xs_new, ffn_p, ffn_s


def kernel(x_prompt, x_sample, state_conv_a, cache_win_k, cache_win_v, state_conv_c, state_ssm, state_ffn_conv,
           norm_mix_e, w_in_e, conv_a_w, conv_a_b, ln_a_g, ln_a_b, q_norm_g, k_norm_g, sinks, w_out_e,
           norm_mix_o, w_in_o, conv_c_w, conv_c_b, dt_bias, a_log, d_skip, gnorm_c, w_out_o,
           norm_ffn, w_gate, w_up, ffn_conv_w, ffn_conv_b, w_down):
    bp, tp, _ = x_prompt.shape
    nb, ts, _ = x_sample.shape
    depth = norm_ffn.shape[0]
    n_odd = state_ssm.shape[0]
    tables = _make_tables(tp, bp, ts, nb)
    dims = dict(bp=bp, tp=tp, nb=nb, ts=ts)

    w_gate, w_up, w_down = _cast_bf16_call(w_gate), _cast_bf16_call(w_up), _cast_bf16_call(w_down)
    w_in_e, w_out_e = _cast_bf16_call(w_in_e), _cast_bf16_call(w_out_e)
    w_in_o, w_out_o = _cast_bf16_call(w_in_o), _cast_bf16_call(w_out_o)
    xp = x_prompt
    xs = jnp.swapaxes(x_sample, 0, 1).reshape(ts * nb, D_MODEL)
    ssm_all = state_ssm.reshape(n_odd, nb, D_INNER, SSM_STATE)
    state_p = state_s = None
    ca_p, ca_s, wk_p, wk_s, wv_p, wv_s = [], [], [], [], [], []
    cc_p, cc_s, ff_p, ff_s = [], [], [], []
    for layer in range(depth):
        i = layer // 2
        if layer % 2 == 0:
            xp, xs, c1, c2, k1, k2, v1, v2 = _even_layer(
                xp, xs, state_conv_a[i], cache_win_k, cache_win_v, i, norm_mix_e[i], w_in_e, conv_a_w[i],
                conv_a_b[i], ln_a_g[i], ln_a_b[i], q_norm_g[i], k_norm_g[i], sinks[i], w_out_e, tables, **dims)
            ca_p.append(c1); ca_s.append(c2); wk_p.append(k1); wk_s.append(k2); wv_p.append(v1); wv_s.append(v2)
        else:
            xp, xs, c1, c2, state_p, state_s = _odd_layer(
                xp, xs, state_conv_c[i], ssm_all, i, state_p, state_s, norm_mix_o[i], w_in_o, conv_c_w[i],
                conv_c_b[i], dt_bias[i], a_log[i], d_skip[i], gnorm_c[i], w_out_o, tables, **dims)
            cc_p.append(c1); cc_s.append(c2)
        xp, xs, f1, f2 = _ffn_layer(xp, xs, state_ffn_conv[layer], layer, norm_ffn[layer], w_gate, w_up,
                                    ffn_conv_w[layer], ffn_conv_b[layer], w_down, bp=bp, nb=nb,
                                    last=layer == depth - 1)
        ff_p.append(f1); ff_s.append(f2)

    y_prompt = xp
    y_sample = jnp.swapaxes(xs.reshape(ts, nb, D_MODEL), 0, 1)
    ssm_p = state_p.reshape(n_odd, bp, SSM_GROUPS, SSM_STATE, HEADS_PER_GROUP, SSM_HEAD_DIM)
    ssm_p = ssm_p.transpose(0, 1, 2, 4, 5, 3).reshape(n_odd, bp, SSM_HEADS, SSM_HEAD_DIM, SSM_STATE)
    ssm_s = state_s.reshape(n_odd, nb, SSM_HEADS, SSM_HEAD_DIM, SSM_STATE)
    return (y_prompt, y_sample,
            jnp.stack(ca_p), jnp.stack(ca_s),
            jnp.stack(wk_p), jnp.stack(wk_s),
            jnp.stack(wv_p), jnp.stack(wv_s),
            jnp.stack(cc_p), jnp.stack(cc_s),
            ssm_p, ssm_s,
            jnp.stack(ff_p), jnp.stack(ff_s))
```
